```python
import jax, jax.numpy as jnp
from jax import lax
import numpy as np

D_MODEL = 4096
BATCH = 2
SEQ = 4096
DEPTH = 2

N_MIXERS = 2
RMS_EPS = 1e-6

GLA_HEADS = 4
GLA_DK = D_MODEL // 2
GLA_DV = D_MODEL
GLA_HEAD_K = GLA_DK // GLA_HEADS
GLA_HEAD_V = GLA_DV // GLA_HEADS
GLA_GATE_RANK = 16
GLA_GATE_TAU = 16.0
GLA_CHUNK = 64
GLA_IN_COLS = 2 * GLA_DK + 2 * GLA_DV + GLA_GATE_RANK

SB_HEADS = 32
SB_HEAD_DIM = D_MODEL // SB_HEADS
SB_BLOCK = 128

D_FF_DENSE = 256 * ((8 * D_MODEL // 3 + 255) // 256)
N_EXPERTS = 8
TOP_K = 2
D_FF_EXPERT = D_FF_DENSE // 2

N_GLA_LAYERS = (DEPTH + 1) // 2
N_SB_LAYERS = DEPTH // 2
N_DENSE_LAYERS = (DEPTH + 1) // 2
N_MOE_LAYERS = DEPTH // 2

kernel_name = "gla_stickbreaking_moe_hybrid"


def rms_norm(x, g):
    xf = x.astype(jnp.float32)
    y = xf * lax.rsqrt(jnp.mean(xf * xf, axis=-1, keepdims=True) + RMS_EPS)
    return (y * g.astype(jnp.float32)).astype(x.dtype)


def gla_mixer(h, w_in, w_gate, b_gate, o_gain, w_out):
    B, S, _ = h.shape
    n_chunks = S // GLA_CHUNK
    proj = h @ w_in
    q, k, v, r, a_low = jnp.split(
        proj, [GLA_DK, 2 * GLA_DK, 2 * GLA_DK + GLA_DV, 2 * GLA_DK + 2 * GLA_DV], axis=-1)
    log_alpha = jax.nn.log_sigmoid((a_low @ w_gate + b_gate).astype(jnp.float32)) / GLA_GATE_TAU

    def to_chunks(t, hd):
        t = t.astype(jnp.float32).reshape(B, n_chunks, GLA_CHUNK, GLA_HEADS, hd)
        return jnp.transpose(t, (1, 0, 3, 2, 4))

    qc = to_chunks(q, GLA_HEAD_K) * (GLA_HEAD_K ** -0.5)
    kc = to_chunks(k, GLA_HEAD_K)
    vc = to_chunks(v, GLA_HEAD_V)
    gc = to_chunks(log_alpha, GLA_HEAD_K)
    causal = jnp.tril(jnp.ones((GLA_CHUNK, GLA_CHUNK), dtype=bool))[:, :, None]

    def step(state, inp):
        q_, k_, v_, g_ = inp
        b = jnp.cumsum(g_, axis=-2)
        diff = b[:, :, :, None, :] - b[:, :, None, :, :]
        decay = jnp.exp(jnp.where(causal, diff, -jnp.inf))
        scores = jnp.einsum('bhtk,bhsk,bhtsk->bhts', q_, k_, decay)
        o = (jnp.einsum('bhts,bhsv->bhtv', scores, v_)
             + jnp.einsum('bhtk,bhkv->bhtv', q_ * jnp.exp(b), state))
        b_last = b[:, :, -1:, :]
        state = (jnp.exp(b_last[:, :, 0, :, None]) * state
                 + jnp.einsum('bhsk,bhsv->bhkv', k_ * jnp.exp(b_last - b), v_))
        return state, o

    state0 = jnp.zeros((B, GLA_HEADS, GLA_HEAD_K, GLA_HEAD_V), jnp.float32)
    _, o = lax.scan(step, state0, (qc, kc, vc, gc))
    o = jnp.transpose(o, (1, 0, 3, 2, 4)).reshape(B, S, GLA_HEADS, GLA_HEAD_V)
    o = o * lax.rsqrt(jnp.mean(o * o, axis=-1, keepdims=True) + RMS_EPS) * o_gain.astype(jnp.float32)
    o = o.reshape(B, S, GLA_DV) * jax.nn.silu(r.astype(jnp.float32))
    return o.astype(h.dtype) @ w_out


def sb_mixer(h, w_in, w_out):
    B, S, _ = h.shape
    nb = S // SB_BLOCK
    q, k, v = jnp.split(h @ w_in, 3, axis=-1)

    def heads(t):
        return jnp.transpose(t.reshape(B, S, SB_HEADS, SB_HEAD_DIM), (0, 2, 1, 3))

    q, k, v = heads(q), heads(k), heads(v)
    q_blocks = jnp.transpose(q.reshape(B, SB_HEADS, nb, SB_BLOCK, SB_HEAD_DIM), (2, 0, 1, 3, 4))
    key_pos = jnp.arange(S)
    scale = SB_HEAD_DIM ** -0.5

    def block(args):
        q_blk, blk = args
        z = jnp.einsum('bhqd,bhkd->bhqk', q_blk, k).astype(jnp.float32) * scale
        q_pos = blk * SB_BLOCK + jnp.arange(SB_BLOCK)
        strict = key_pos[None, :] < q_pos[:, None]
        log_1m = jnp.where(strict, jax.nn.log_sigmoid(-z), 0.0)
        suffix = lax.cumsum(log_1m, axis=3, reverse=True) - log_1m
        w = jnp.where(strict, jnp.exp(jax.nn.log_sigmoid(z) + suffix), 0.0)
        return jnp.einsum('bhqk,bhkd->bhqd', w.astype(v.dtype), v)

    o = lax.map(block, (q_blocks, jnp.arange(nb)))
    o = jnp.transpose(o, (1, 0, 3, 2, 4)).reshape(B, S, SB_HEADS * SB_HEAD_DIM)
    return o @ w_out


def swiglu(h, w_in, w_out):
    g, u = jnp.split(h @ w_in, 2, axis=-1)
    return (jax.nn.silu(g) * u) @ w_out


def moe_ffn(h, w_router, w_in, w_out):
    B, S, D = h.shape
    t = h.reshape(B * S, D)
    logits = (t @ w_router).astype(jnp.float32)
    top_val, top_idx = lax.top_k(logits, TOP_K)
    gates = jax.nn.softmax(top_val, axis=-1)
    combine = jnp.sum(jax.nn.one_hot(top_idx, N_EXPERTS, dtype=jnp.float32) * gates[..., None], axis=1)
    out = jnp.zeros_like(t)
    for e in range(N_EXPERTS):
        y = swiglu(t, w_in[e], w_out[e])
        out = out + combine[:, e:e + 1].astype(y.dtype) * y
    return out.reshape(B, S, D)


def setup_inputs(seed: int = 0) -> dict:
    key = jax.random.key(seed)
    ks = jax.random.split(key, 16)
    f32 = jnp.float32

    def nrm(k, shape, fan_in):
        return jax.random.normal(k, shape, f32) * (fan_in ** -0.5)

    return {
        "x": jax.random.normal(ks[0], (BATCH, SEQ, D_MODEL), f32),
        "attn_norm": 1.0 + 0.01 * jax.random.normal(ks[1], (DEPTH, D_MODEL), f32),
        "ffn_norm": 1.0 + 0.01 * jax.random.normal(ks[2], (DEPTH, D_MODEL), f32),
        "gla_w_in": nrm(ks[3], (N_GLA_LAYERS, D_MODEL, GLA_IN_COLS), D_MODEL),
        "gla_w_gate": nrm(ks[4], (N_GLA_LAYERS, GLA_GATE_RANK, GLA_DK), GLA_GATE_RANK),
        "gla_b_gate": 0.1 * jax.random.normal(ks[5], (N_GLA_LAYERS, GLA_DK), f32),
        "gla_onorm": 1.0 + 0.01 * jax.random.normal(ks[6], (N_GLA_LAYERS, GLA_HEAD_V), f32),
        "gla_w_out": nrm(ks[7], (N_GLA_LAYERS, GLA_DV, D_MODEL), GLA_DV),
        "sb_w_in": nrm(ks[8], (N_SB_LAYERS, D_MODEL, 3 * D_MODEL), D_MODEL),
        "sb_w_out": nrm(ks[9], (N_SB_LAYERS, D_MODEL, D_MODEL), D_MODEL),
        "dense_w_in": nrm(ks[10], (N_DENSE_LAYERS, D_MODEL, 2 * D_FF_DENSE), D_MODEL),
        "dense_w_out": nrm(ks[11], (N_DENSE_LAYERS, D_FF_DENSE, D_MODEL), D_FF_DENSE),
        "moe_router": nrm(ks[12], (N_MOE_LAYERS, D_MODEL, N_EXPERTS), D_MODEL),
        "moe_w_in": nrm(ks[13], (N_MOE_LAYERS, N_EXPERTS, D_MODEL, 2 * D_FF_EXPERT), D_MODEL),
        "moe_w_out": nrm(ks[14], (N_MOE_LAYERS, N_EXPERTS, D_FF_EXPERT, D_MODEL), D_FF_EXPERT),
        "final_norm": 1.0 + 0.01 * jax.random.normal(ks[15], (D_MODEL,), f32),
    }


def reference(x, attn_norm, ffn_norm, gla_w_in, gla_w_gate, gla_b_gate, gla_onorm, gla_w_out,
              sb_w_in, sb_w_out, dense_w_in, dense_w_out, moe_router, moe_w_in, moe_w_out,
              final_norm):
    for i in range(DEPTH):
        m = i // N_MIXERS
        h = rms_norm(x, attn_norm[i])
        if i % N_MIXERS == 0:
            x = x + gla_mixer(h, gla_w_in[m], gla_w_gate[m], gla_b_gate[m], gla_onorm[m], gla_w_out[m])
        else:
            x = x + sb_mixer(h, sb_w_in[m], sb_w_out[m])
        f = i // 2
        h = rms_norm(x, ffn_norm[i])
        if i % 2 == 0:
            x = x + swiglu(h, dense_w_in[f], dense_w_out[f])
        else:
            x = x + moe_ffn(h, moe_router[f], moe_w_in[f], moe_w_out[f])
    return rms_norm(x, final_norm)
```

```python
import functools

import jax
import jax.numpy as jnp
from jax import lax
from jax.experimental import pallas as pl
from jax.experimental.pallas import tpu as pltpu

F32 = jnp.float32
BF16 = jnp.bfloat16

RMS_EPS = 1e-6
GLA_HEADS = 4
GLA_GATE_TAU = 16.0
GLA_CHUNK = 64
GLA_SUB = 16
SB_HEADS = 32
SB_TQ = 512
SB_TK = 256
TOP_K = 2
LANES = 128
VMEM_LIMIT_BYTES = 56 * 1024 * 1024

MM_TM = 1024
MM_TN = 512
MOE_TM = 512
MOE_IN_TN = 128
MOE_OUT_TN = 512
GATHER_ROWS = 256
COMBINE_ROWS = 128


def _cparams(n_axes):
    return pltpu.CompilerParams(
        dimension_semantics=("arbitrary",) * n_axes,
        vmem_limit_bytes=VMEM_LIMIT_BYTES)


def _silu(x):
    return x / (1.0 + jnp.exp(-x))


def _log_sigmoid(x):
    return jnp.minimum(x, 0.0) - jnp.log1p(jnp.exp(-jnp.abs(x)))


def _rmsnorm_kernel(x_ref, g_ref, o_ref):
    x = x_ref[...]
    ms = jnp.mean(x * x, axis=-1, keepdims=True)
    o_ref[...] = (x * lax.rsqrt(ms + RMS_EPS) * g_ref[...]).astype(o_ref.dtype)


def _rmsnorm(x, g, out_dtype):
    m, d = x.shape
    tm = min(256, m)
    return pl.pallas_call(
        _rmsnorm_kernel,
        grid=(m // tm,),
        in_specs=[pl.BlockSpec((tm, d), lambda i: (i, 0)),
                  pl.BlockSpec((1, d), lambda i: (0, 0))],
        out_specs=pl.BlockSpec((tm, d), lambda i: (i, 0)),
        out_shape=jax.ShapeDtypeStruct((m, d), out_dtype),
        compiler_params=_cparams(1),
        name="rmsnorm",
    )(x, g.reshape(1, d))


def _mm_plain_kernel(x_ref, w_ref, o_ref):
    acc = jnp.dot(x_ref[...], w_ref[...].astype(BF16), preferred_element_type=F32)
    o_ref[...] = acc.astype(o_ref.dtype)


def _mm_res_kernel(x_ref, w_ref, res_ref, o_ref):
    acc = jnp.dot(x_ref[...], w_ref[...].astype(BF16), preferred_element_type=F32)
    o_ref[...] = res_ref[...] + acc


def _mm_swiglu_kernel(x_ref, wg_ref, wu_ref, o_ref):
    x = x_ref[...]
    g = jnp.dot(x, wg_ref[...].astype(BF16), preferred_element_type=F32)
    u = jnp.dot(x, wu_ref[...].astype(BF16), preferred_element_type=F32)
    o_ref[...] = (_silu(g) * u).astype(o_ref.dtype)


def _matmul(x, w, *, n_cols, tk, k_blk=0, w_col_blk=0, tm=None, tn=None,
            out_dtype=BF16, res=None, name="matmul"):
    m = x.shape[0]
    tm = min(tm or MM_TM, m)
    tn = min(tn or MM_TN, n_cols)
    assert m % tm == 0 and n_cols % tn == 0
    grid = (m // tm, n_cols // tn)
    x_spec = pl.BlockSpec((tm, tk), lambda i, j: (i, k_blk))
    w_spec = pl.BlockSpec((tk, tn), lambda i, j: (k_blk, w_col_blk + j))
    o_spec = pl.BlockSpec((tm, tn), lambda i, j: (i, j))
    if res is None:
        kern, in_specs, args = _mm_plain_kernel, [x_spec, w_spec], (x, w)
    else:
        kern, in_specs, args = _mm_res_kernel, [x_spec, w_spec, o_spec], (x, w, res)
        out_dtype = F32
    return pl.pallas_call(
        kern, grid=grid, in_specs=in_specs, out_specs=o_spec,
        out_shape=jax.ShapeDtypeStruct((m, n_cols), out_dtype),
        compiler_params=_cparams(2), name=name,
    )(*args)


def _swiglu_in(x, w, d_ff, *, tm=None, tn=None, name="swiglu_in"):
    m, k = x.shape
    tm = min(tm or MM_TM, m)
    tn = tn or 256
    assert m % tm == 0 and d_ff % tn == 0
    nb = d_ff // tn
    return pl.pallas_call(
        _mm_swiglu_kernel,
        grid=(m // tm, nb),
        in_specs=[pl.BlockSpec((tm, k), lambda i, j: (i, 0)),
                  pl.BlockSpec((k, tn), lambda i, j: (0, j)),
                  pl.BlockSpec((k, tn), lambda i, j: (0, nb + j))],
        out_specs=pl.BlockSpec((tm, tn), lambda i, j: (i, j)),
        out_shape=jax.ShapeDtypeStruct((m, d_ff), BF16),
        compiler_params=_cparams(2), name=name,
    )(x, w, w)


def _dot_f32(a, b):
    return jnp.dot(a, b, preferred_element_type=F32, precision=lax.Precision.HIGHEST)


def _gla_gate_kernel(h_ref, wa_ref, wg_ref, bg_ref, o_ref, *, chunk):
    tm = h_ref.shape[0]
    a_low = jnp.dot(h_ref[...], wa_ref[...], preferred_element_type=F32)
    xg = _dot_f32(a_low, wg_ref[...]) + bg_ref[...]
    log_alpha = _log_sigmoid(xg) * (1.0 / GLA_GATE_TAU)
    row = lax.broadcasted_iota(jnp.int32, (tm, tm), 0)
    col = lax.broadcasted_iota(jnp.int32, (tm, tm), 1)
    same_chunk = (row // chunk) == (col // chunk)
    tril = jnp.where(same_chunk & (col <= row), 1.0, 0.0).astype(F32)
    o_ref[...] = _dot_f32(tril, log_alpha)


def _gla_gate(h, w_a, w_gate, b_gate, chunk):
    m, d = h.shape
    rank, dk = w_gate.shape
    tm = min(256, m)
    wa_pad = jnp.zeros((d, LANES), BF16).at[:, :rank].set(w_a.astype(BF16))
    wg_pad = jnp.zeros((LANES, dk), F32).at[:rank, :].set(w_gate)
    return pl.pallas_call(
        functools.partial(_gla_gate_kernel, chunk=chunk),
        grid=(m // tm,),
        in_specs=[pl.BlockSpec((tm, d), lambda i: (i, 0)),
                  pl.BlockSpec((d, LANES), lambda i: (0, 0)),
                  pl.BlockSpec((LANES, dk), lambda i: (0, 0)),
                  pl.BlockSpec((1, dk), lambda i: (0, 0))],
        out_specs=pl.BlockSpec((tm, dk), lambda i: (i, 0)),
        out_shape=jax.ShapeDtypeStruct((m, dk), F32),
        compiler_params=_cparams(1), name="gla_gate",
    )(h, wa_pad, wg_pad, b_gate.reshape(1, dk))


def _gla_core_kernel(q_ref, k_ref, v_ref, r_ref, b_ref, gain_ref, o_ref, s_ref, *, scale, sub):
    chunk, hk = q_ref.shape

    @pl.when(pl.program_id(2) == 0)
    def _():
        s_ref[...] = jnp.zeros_like(s_ref)

    b = b_ref[...]
    q = q_ref[...].astype(F32) * scale
    k = k_ref[...].astype(F32)
    v = v_ref[...]
    state = s_ref[...]

    o = jnp.dot((q * jnp.exp(b)).astype(BF16), state.astype(BF16), preferred_element_type=F32)

    row_id = lax.broadcasted_iota(jnp.int32, (sub, chunk), 0)
    key_id = lax.broadcasted_iota(jnp.int32, (sub, chunk), 1)
    score_rows = []
    for i in range(chunk // sub):
        lo = i * sub
        b_i = b[lo:lo + sub, :]
        q_i = q[lo:lo + sub, :]
        diag = jnp.zeros((sub, chunk), F32)
        for j in range(sub):
            s = lo + j
            decay = jnp.exp(jnp.minimum(b_i - b[s:s + 1, :], 0.0))
            col = jnp.sum(q_i * decay * k[s:s + 1, :], axis=-1, keepdims=True)
            diag = jnp.where(key_id == s, col, diag)
        scores = jnp.where(key_id - lo <= row_id, diag, 0.0)
        if i > 0:
            b_first = b[lo:lo + 1, :]
            q_t = (q_i * jnp.exp(b_i - b_first)).astype(BF16)
            k_t = (k * jnp.exp(jnp.minimum(b_first - b, 0.0))).astype(BF16)
            below = lax.dot_general(q_t, k_t, (((1,), (1,)), ((), ())), preferred_element_type=F32)
            scores = jnp.where(key_id < lo, below, scores)
        score_rows.append(scores)
    scores = jnp.concatenate(score_rows, axis=0).astype(BF16)
    o = o + jnp.dot(scores, v, preferred_element_type=F32)

    b_last = b[chunk - 1:chunk, :]
    k_state = (k * jnp.exp(b_last - b)).astype(BF16)
    update = lax.dot_general(k_state, v, (((0,), (0,)), ((), ())), preferred_element_type=F32)
    decay_rows = jnp.broadcast_to(jnp.exp(b_last), (LANES, hk))
    decay_col = jnp.transpose(decay_rows)[:, 0:1]
    s_ref[...] = state * decay_col + update

    ms = jnp.mean(o * o, axis=-1, keepdims=True)
    o = o * lax.rsqrt(ms + RMS_EPS) * gain_ref[...]
    o_ref[...] = (o * _silu(r_ref[...].astype(F32))).astype(o_ref.dtype)


def _gla_core(proj, b, gain, batch, seq, heads, dk, dv):
    m = batch * seq
    hk, hv = dk // heads, dv // heads
    chunk = min(GLA_CHUNK, seq)
    nc = seq // chunk
    k_blk0 = dk // hk
    v_blk0 = (2 * dk) // hv
    r_blk0 = (2 * dk + dv) // hv

    def rows(bi, ci):
        return bi * nc + ci

    return pl.pallas_call(
        functools.partial(_gla_core_kernel, scale=float(hk) ** -0.5, sub=min(GLA_SUB, chunk)),
        grid=(batch, heads, nc),
        in_specs=[pl.BlockSpec((chunk, hk), lambda bi, h, c: (rows(bi, c), h)),
                  pl.BlockSpec((chunk, hk), lambda bi, h, c: (rows(bi, c), k_blk0 + h)),
                  pl.BlockSpec((chunk, hv), lambda bi, h, c: (rows(bi, c), v_blk0 + h)),
                  pl.BlockSpec((chunk, hv), lambda bi, h, c: (rows(bi, c), r_blk0 + h)),
                  pl.BlockSpec((chunk, hk), lambda bi, h, c: (rows(bi, c), h)),
                  pl.BlockSpec((1, hv), lambda bi, h, c: (0, 0))],
        out_specs=pl.BlockSpec((chunk, hv), lambda bi, h, c: (rows(bi, c), h)),
        out_shape=jax.ShapeDtypeStruct((m, dv), BF16),
        scratch_shapes=[pltpu.VMEM((hk, hv), F32)],
        compiler_params=_cparams(3), name="gla_core",
    )(proj, proj, proj, proj, b, gain.reshape(1, hv))


def _sb_kernel(q_ref, k_ref, v_ref, o_ref, acc_ref, carry_ref, *, scale, tk):
    tq, dh = q_ref.shape
    ratio = tq // tk
    qi = pl.program_id(2)
    q = (q_ref[...].astype(F32) * scale).astype(BF16)

    later = lax.broadcasted_iota(jnp.int32, (2 * tk, tk), 0) % tk
    key = lax.broadcasted_iota(jnp.int32, (2 * tk, tk), 1)
    suffix_ones = jnp.where(later > key, 1.0, 0.0).astype(BF16)

    acc_ref[...] = jnp.zeros_like(acc_ref)
    carry_ref[...] = jnp.zeros_like(carry_ref)

    def tile(k_start, strict):
        k_blk = k_ref[pl.ds(k_start, tk), :]
        v_blk = v_ref[pl.ds(k_start, tk), :]
        z = lax.dot_general(q, k_blk, (((1,), (1,)), ((), ())), preferred_element_type=F32)
        ls_pos = _log_sigmoid(z)
        log_1m = ls_pos - z
        if strict is not None:
            log_1m = jnp.where(strict, log_1m, 0.0)
        hi = log_1m.astype(BF16)
        lo = (log_1m - hi.astype(F32)).astype(BF16)
        suffix = jnp.dot(jnp.concatenate([hi, lo], axis=1), suffix_ones, preferred_element_type=F32)
        w = jnp.exp(ls_pos + suffix + carry_ref[...])
        if strict is not None:
            w = jnp.where(strict, w, 0.0)
        acc_ref[...] += jnp.dot(w.astype(BF16), v_blk, preferred_element_type=F32)
        carry_ref[...] += (suffix + log_1m)[:, 0:1]

    row = lax.broadcasted_iota(jnp.int32, (tq, tk), 0)
    lane = lax.broadcasted_iota(jnp.int32, (tq, tk), 1)
    for d in range(ratio):
        off = (ratio - 1 - d) * tk
        tile(pl.multiple_of(qi * tq + off, tk), (lane + off) < row)

    def body(t, _):
        tile(pl.multiple_of(qi * tq - (t + 1) * tk, tk), None)
        return 0

    lax.fori_loop(0, qi * ratio, body, 0)
    o_ref[...] = acc_ref[...].astype(o_ref.dtype)


def _sb_core(qkv, batch, seq, heads, d_model):
    dh = d_model // heads
    tq = min(SB_TQ, seq)
    tk = min(SB_TK, tq)
    nq = seq // tq
    return pl.pallas_call(
        functools.partial(_sb_kernel, scale=float(dh) ** -0.5, tk=tk),
        grid=(batch, heads, nq),
        in_specs=[pl.BlockSpec((tq, dh), lambda b, h, i: (b * nq + i, h)),
                  pl.BlockSpec((seq, dh), lambda b, h, i: (b, heads + h)),
                  pl.BlockSpec((seq, dh), lambda b, h, i: (b, 2 * heads + h))],
        out_specs=pl.BlockSpec((tq, dh), lambda b, h, i: (b * nq + i, h)),
        out_shape=jax.ShapeDtypeStruct((batch * seq, d_model), BF16),
        scratch_shapes=[pltpu.VMEM((tq, dh), F32), pltpu.VMEM((tq, 1), F32)],
        compiler_params=_cparams(3), name="sb_core",
    )(qkv, qkv, qkv)


def _router_kernel(x_ref, g_ref, wr_ref, h_ref, top_ref, *, n_experts):
    x = x_ref[...]
    ms = jnp.mean(x * x, axis=-1, keepdims=True)
    h = x * lax.rsqrt(ms + RMS_EPS) * g_ref[...]
    h_ref[...] = h
    logits = _dot_f32(h, wr_ref[...])
    lane = lax.broadcasted_iota(jnp.int32, logits.shape, 1)
    logits = jnp.where(lane < n_experts, logits, -jnp.inf)
    v1 = jnp.max(logits, axis=-1, keepdims=True)
    i1 = jnp.min(jnp.where(logits == v1, lane, LANES), axis=-1, keepdims=True)
    rest = jnp.where(lane == i1, -jnp.inf, logits)
    v2 = jnp.max(rest, axis=-1, keepdims=True)
    i2 = jnp.min(jnp.where(rest == v2, lane, LANES), axis=-1, keepdims=True)
    e = jnp.exp(v2 - v1)
    g1 = 1.0 / (1.0 + e)
    g2 = e * g1
    out = jnp.where(lane == 0, i1.astype(F32),
                    jnp.where(lane == 1, i2.astype(F32),
                              jnp.where(lane == 2, g1, jnp.where(lane == 3, g2, 0.0))))
    top_ref[...] = out


def _router(x, g, w_router):
    m, d = x.shape
    n_experts = w_router.shape[1]
    tm = min(256, m)
    wr_pad = jnp.zeros((d, LANES), F32).at[:, :n_experts].set(w_router)
    return pl.pallas_call(
        functools.partial(_router_kernel, n_experts=n_experts),
        grid=(m // tm,),
        in_specs=[pl.BlockSpec((tm, d), lambda i: (i, 0)),
                  pl.BlockSpec((1, d), lambda i: (0, 0)),
                  pl.BlockSpec((d, LANES), lambda i: (0, 0))],
        out_specs=[pl.BlockSpec((tm, d), lambda i: (i, 0)),
                   pl.BlockSpec((tm, LANES), lambda i: (i, 0))],
        out_shape=[jax.ShapeDtypeStruct((m, d), F32),
                   jax.ShapeDtypeStruct((m, LANES), F32)],
        compiler_params=_cparams(1), name="moe_router",
    )(x, g.reshape(1, d), wr_pad)


def _row_copy(src_hbm, dst_ref, sem, src_row, dst_row):
    return pltpu.make_async_copy(src_hbm.at[pl.ds(src_row, 1), :],
                                 dst_ref.at[pl.ds(dst_row, 1), :], sem)


def _gather_kernel(idx_ref, src_hbm, o_ref, sem):
    rows = o_ref.shape[0]
    base = pl.program_id(0) * rows

    def start(r, _):
        _row_copy(src_hbm, o_ref, sem, idx_ref[base + r], r).start()
        return 0

    def wait(r, _):
        _row_copy(src_hbm, o_ref, sem, 0, r).wait()
        return 0

    lax.fori_loop(0, rows, start, 0)
    lax.fori_loop(0, rows, wait, 0)


def _gather_rows(src, idx):
    n = idx.shape[0]
    d = src.shape[1]
    rows = min(GATHER_ROWS, n)
    assert n % rows == 0
    return pl.pallas_call(
        _gather_kernel,
        grid_spec=pltpu.PrefetchScalarGridSpec(
            num_scalar_prefetch=1, grid=(n // rows,),
            in_specs=[pl.BlockSpec(memory_space=pl.ANY)],
            out_specs=pl.BlockSpec((rows, d), lambda i, idx: (i, 0)),
            scratch_shapes=[pltpu.SemaphoreType.DMA(())]),
        out_shape=jax.ShapeDtypeStruct((n, d), src.dtype),
        compiler_params=_cparams(1), name="moe_gather",
    )(idx, src)


def _moe_in_kernel(eid_ref, nvalid_ref, x_ref, wg_ref, wu_ref, o_ref, xb_ref, wb_ref):
    i, j = pl.program_id(0), pl.program_id(1)
    tn = o_ref.shape[1]

    @pl.when(i < nvalid_ref[0])
    def _():
        @pl.when(j == 0)
        def _():
            xb_ref[...] = x_ref[...].astype(BF16)
        wb_ref[:, :tn] = wg_ref[...].astype(BF16)
        wb_ref[:, tn:] = wu_ref[...].astype(BF16)
        gu = jnp.dot(xb_ref[...], wb_ref[...], preferred_element_type=F32)
        o_ref[...] = (_silu(gu[:, :tn]) * gu[:, tn:]).astype(o_ref.dtype)

    @pl.when(i >= nvalid_ref[0])
    def _():
        o_ref[...] = jnp.zeros_like(o_ref)


def _moe_in(xs, w_in, tile_expert, n_valid, tm):
    r, d = xs.shape
    d_ff = w_in.shape[2] // 2
    tn = MOE_IN_TN
    nb = d_ff // tn
    assert d_ff % tn == 0

    def x_map(i, j, eid, nv):
        return (jnp.minimum(i, nv[0] - 1), 0)

    return pl.pallas_call(
        _moe_in_kernel,
        grid_spec=pltpu.PrefetchScalarGridSpec(
            num_scalar_prefetch=2, grid=(r // tm, nb),
            in_specs=[pl.BlockSpec((tm, d), x_map),
                      pl.BlockSpec((None, d, tn), lambda i, j, eid, nv: (eid[i], 0, j)),
                      pl.BlockSpec((None, d, tn), lambda i, j, eid, nv: (eid[i], 0, nb + j))],
            out_specs=pl.BlockSpec((tm, tn), lambda i, j, eid, nv: (i, j)),
            scratch_shapes=[pltpu.VMEM((tm, d), BF16), pltpu.VMEM((d, 2 * tn), BF16)]),
        out_shape=jax.ShapeDtypeStruct((r, d_ff), BF16),
        compiler_params=_cparams(2), name="moe_in",
    )(tile_expert, n_valid, xs, w_in, w_in)


def _moe_out_kernel(eid_ref, nvalid_ref, x_ref, w_ref, s_ref, o_ref):
    i = pl.program_id(0)

    @pl.when(i < nvalid_ref[0])
    def _():
        acc = jnp.dot(x_ref[...], w_ref[...].astype(BF16), preferred_element_type=F32)
        o_ref[...] = acc * s_ref[...]

    @pl.when(i >= nvalid_ref[0])
    def _():
        o_ref[...] = jnp.zeros_like(o_ref)


def _moe_out(act, w_out, row_gate, tile_expert, n_valid, tm):
    r, d_ff = act.shape
    d = w_out.shape[2]
    tn = min(MOE_OUT_TN, d)

    def x_map(i, j, eid, nv):
        return (jnp.minimum(i, nv[0] - 1), 0)

    return pl.pallas_call(
        _moe_out_kernel,
        grid_spec=pltpu.PrefetchScalarGridSpec(
            num_scalar_prefetch=2, grid=(r // tm, d // tn),
            in_specs=[pl.BlockSpec((tm, d_ff), x_map),
                      pl.BlockSpec((None, d_ff, tn), lambda i, j, eid, nv: (eid[i], 0, j)),
                      pl.BlockSpec((tm, 1), x_map)],
            out_specs=pl.BlockSpec((tm, tn), lambda i, j, eid, nv: (i, j))),
        out_shape=jax.ShapeDtypeStruct((r, d), F32),
        compiler_params=_cparams(2), name="moe_out",
    )(tile_expert, n_valid, act, w_out, row_gate.reshape(r, 1))


def _combine_kernel(dest_ref, x_ref, y_hbm, g_ref, o_ref, a_ref, b_ref, sem):
    rows = x_ref.shape[0]
    base = pl.program_id(0) * rows

    def start(r, _):
        _row_copy(y_hbm, a_ref, sem, dest_ref[2 * (base + r)], r).start()
        _row_copy(y_hbm, b_ref, sem, dest_ref[2 * (base + r) + 1], r).start()
        return 0

    def wait(r, _):
        _row_copy(y_hbm, a_ref, sem, 0, r).wait()
        _row_copy(y_hbm, b_ref, sem, 0, r).wait()
        return 0

    lax.fori_loop(0, rows, start, 0)
    lax.fori_loop(0, rows, wait, 0)
    x = x_ref[...] + (a_ref[...] + b_ref[...])
    ms = jnp.mean(x * x, axis=-1, keepdims=True)
    o_ref[...] = x * lax.rsqrt(ms + RMS_EPS) * g_ref[...]


def _combine_norm(x, y, dest, g):
    m, d = x.shape
    rows = min(COMBINE_ROWS, m)
    return pl.pallas_call(
        _combine_kernel,
        grid_spec=pltpu.PrefetchScalarGridSpec(
            num_scalar_prefetch=1, grid=(m // rows,),
            in_specs=[pl.BlockSpec((rows, d), lambda i, dest: (i, 0)),
                      pl.BlockSpec(memory_space=pl.ANY),
                      pl.BlockSpec((1, d), lambda i, dest: (0, 0))],
            out_specs=pl.BlockSpec((rows, d), lambda i, dest: (i, 0)),
            scratch_shapes=[pltpu.VMEM((rows, d), F32), pltpu.VMEM((rows, d), F32),
                            pltpu.SemaphoreType.DMA(())]),
        out_shape=jax.ShapeDtypeStruct((m, d), F32),
        compiler_params=_cparams(1), name="moe_combine_norm",
    )(dest, x, y, g.reshape(1, d))


def _dispatch_plan(top, n_experts, tm):
    m = top.shape[0]
    n_pairs = m * TOP_K
    expert = top[:, :TOP_K].astype(jnp.int32).reshape(n_pairs)
    gate = top[:, TOP_K:2 * TOP_K].reshape(n_pairs)
    onehot = (expert[:, None] == jnp.arange(n_experts, dtype=jnp.int32)[None, :]).astype(jnp.int32)
    before = jnp.cumsum(onehot, axis=0) - onehot
    rank = jnp.sum(before * onehot, axis=1)
    counts = jnp.sum(onehot, axis=0)
    tiles = (counts + tm - 1) // tm
    tile_end = jnp.cumsum(tiles)
    group_start = (tile_end - tiles) * tm
    dest = group_start[expert] + rank
    n_tiles = n_pairs // tm + n_experts
    n_rows = n_tiles * tm
    src_token = jnp.zeros((n_rows,), jnp.int32).at[dest].set(jnp.arange(n_pairs, dtype=jnp.int32) // TOP_K)
    row_gate = jnp.zeros((n_rows,), F32).at[dest].set(gate)
    tile_expert = jnp.minimum(
        jnp.searchsorted(tile_end, jnp.arange(n_tiles, dtype=jnp.int32), side="right"),
        n_experts - 1).astype(jnp.int32)
    n_valid = tile_end[-1:].astype(jnp.int32)
    return dest.astype(jnp.int32), src_token, row_gate, tile_expert, n_valid


def kernel(x, attn_norm, ffn_norm, gla_w_in, gla_w_gate, gla_b_gate, gla_onorm, gla_w_out,
           sb_w_in, sb_w_out, dense_w_in, dense_w_out, moe_router, moe_w_in, moe_w_out,
           final_norm):
    batch, seq, d = x.shape
    m = batch * seq
    x = x.reshape(m, d)

    rank, dk = gla_w_gate.shape[1:]
    hv = gla_onorm.shape[1]
    dv = GLA_HEADS * hv
    n_proj = 2 * dk + 2 * dv
    h = _rmsnorm(x, attn_norm[0], BF16)
    proj = _matmul(h, gla_w_in[0], n_cols=n_proj, tk=d, name="gla_in")
    b = _gla_gate(h, gla_w_in[0][:, n_proj:], gla_w_gate[0], gla_b_gate[0], min(GLA_CHUNK, seq))
    o = _gla_core(proj, b, gla_onorm[0], batch, seq, GLA_HEADS, dk, dv)
    x = _matmul(o, gla_w_out[0], n_cols=d, tk=dv, res=x, name="gla_out")

    d_ff = dense_w_out.shape[1]
    h = _rmsnorm(x, ffn_norm[0], BF16)
    act = _swiglu_in(h, dense_w_in[0], d_ff, name="dense_in")
    half = d_ff // 2
    x = _matmul(act, dense_w_out[0], n_cols=d, tk=half, k_blk=0, tn=256, res=x, name="dense_out0")
    x = _matmul(act, dense_w_out[0], n_cols=d, tk=half, k_blk=1, tn=256, res=x, name="dense_out1")

    h = _rmsnorm(x, attn_norm[1], BF16)
    qkv = _matmul(h, sb_w_in[0], n_cols=3 * d, tk=d, name="sb_in")
    o = _sb_core(qkv, batch, seq, SB_HEADS, d)
    x = _matmul(o, sb_w_out[0], n_cols=d, tk=d, res=x, name="sb_out")

    n_experts = moe_router.shape[2]
    tm = min(MOE_TM, m)
    h32, top = _router(x, ffn_norm[1], moe_router[0])
    dest, src_token, row_gate, tile_expert, n_valid = _dispatch_plan(top, n_experts, tm)
    xs = _gather_rows(h32, src_token)
    act = _moe_in(xs, moe_w_in[0], tile_expert, n_valid, tm)
    y = _moe_out(act, moe_w_out[0], row_gate, tile_expert, n_valid, tm)
    out = _combine_norm(x, y, dest, final_norm)
    return out.reshape(batch, seq, d)
```

```python
import functools

import jax
import jax.numpy as jnp
from jax import lax
from jax.experimental import pallas as pl
from jax.experimental.pallas import tpu as pltpu

F32 = jnp.float32
BF16 = jnp.bfloat16

RMS_EPS = 1e-6
LOG2_E = 1.4426950408889634
GLA_HEADS = 4
GLA_GATE_TAU = 16.0
GLA_CHUNK = 64
GLA_SUB = 16
SB_HEADS = 32
SB_TQ = 512
SB_TK = 256
TOP_K = 2
LANES = 128
VMEM_LIMIT_BYTES = 56 * 1024 * 1024

MM_TM = 1024
MM_TN = 512
MOE_TM = 512
MOE_IN_TN = 256
MOE_OUT_TN = 512
GATHER_ROWS = 256
COMBINE_ROWS = 128


def _cparams(n_axes):
    return pltpu.CompilerParams(
        dimension_semantics=("arbitrary",) * n_axes,
        vmem_limit_bytes=VMEM_LIMIT_BYTES)


def _silu(x):
    return x / (1.0 + jnp.exp(-x))


def _log_sigmoid(x):
    return jnp.minimum(x, 0.0) - jnp.log1p(jnp.exp(-jnp.abs(x)))


def _rmsnorm_kernel(x_ref, g_ref, o_ref):
    x = x_ref[...]
    ms = jnp.mean(x * x, axis=-1, keepdims=True)
    o_ref[...] = (x * lax.rsqrt(ms + RMS_EPS) * g_ref[...]).astype(o_ref.dtype)


def _rmsnorm(x, g, out_dtype):
    m, d = x.shape
    tm = min(256, m)
    return pl.pallas_call(
        _rmsnorm_kernel,
        grid=(m // tm,),
        in_specs=[pl.BlockSpec((tm, d), lambda i: (i, 0)),
                  pl.BlockSpec((1, d), lambda i: (0, 0))],
        out_specs=pl.BlockSpec((tm, d), lambda i: (i, 0)),
        out_shape=jax.ShapeDtypeStruct((m, d), out_dtype),
        compiler_params=_cparams(1),
        name="rmsnorm",
    )(x, g.reshape(1, d))


def _mm_plain_kernel(x_ref, w_ref, o_ref):
    acc = jnp.dot(x_ref[...], w_ref[...].astype(BF16), preferred_element_type=F32)
    o_ref[...] = acc.astype(o_ref.dtype)


def _mm_nt_kernel(x_ref, wt_ref, o_ref):
    acc = lax.dot_general(x_ref[...], wt_ref[...].astype(BF16), (((1,), (1,)), ((), ())),
                          preferred_element_type=F32)
    o_ref[...] = acc.astype(o_ref.dtype)


def _matmul_nt(x, wt, *, n_cols, tm=None, tn=None, out_dtype=BF16, name="matmul_nt"):
    m, k = x.shape
    tm = min(tm or MM_TM, m)
    tn = min(tn or MM_TN, n_cols)
    assert m % tm == 0 and n_cols % tn == 0
    return pl.pallas_call(
        _mm_nt_kernel, grid=(m // tm, n_cols // tn),
        in_specs=[pl.BlockSpec((tm, k), lambda i, j: (i, 0)),
                  pl.BlockSpec((tn, k), lambda i, j: (j, 0))],
        out_specs=pl.BlockSpec((tm, tn), lambda i, j: (i, j)),
        out_shape=jax.ShapeDtypeStruct((m, n_cols), out_dtype),
        compiler_params=_cparams(2), name=name,
    )(x, wt)


def _mm_res_kernel(x_ref, w_ref, res_ref, o_ref):
    acc = jnp.dot(x_ref[...], w_ref[...].astype(BF16), preferred_element_type=F32)
    o_ref[...] = res_ref[...] + acc


def _mm_swiglu_kernel(x_ref, wg_ref, wu_ref, o_ref):
    x = x_ref[...]
    g = jnp.dot(x, wg_ref[...].astype(BF16), preferred_element_type=F32)
    u = jnp.dot(x, wu_ref[...].astype(BF16), preferred_element_type=F32)
    o_ref[...] = (_silu(g) * u).astype(o_ref.dtype)


def _matmul(x, w, *, n_cols, tk, k_blk=0, w_col_blk=0, tm=None, tn=None,
            out_dtype=BF16, res=None, name="matmul"):
    m = x.shape[0]
    tm = min(tm or MM_TM, m)
    tn = min(tn or MM_TN, n_cols)
    assert m % tm == 0 and n_cols % tn == 0
    grid = (m // tm, n_cols // tn)
    x_spec = pl.BlockSpec((tm, tk), lambda i, j: (i, k_blk))
    w_spec = pl.BlockSpec((tk, tn), lambda i, j: (k_blk, w_col_blk + j))
    o_spec = pl.BlockSpec((tm, tn), lambda i, j: (i, j))
    if res is None:
        kern, in_specs, args = _mm_plain_kernel, [x_spec, w_spec], (x, w)
    else:
        kern, in_specs, args = _mm_res_kernel, [x_spec, w_spec, o_spec], (x, w, res)
        out_dtype = F32
    return pl.pallas_call(
        kern, grid=grid, in_specs=in_specs, out_specs=o_spec,
        out_shape=jax.ShapeDtypeStruct((m, n_cols), out_dtype),
        compiler_params=_cparams(2), name=name,
    )(*args)


def _swiglu_in(x, w, d_ff, *, tm=None, tn=None, name="swiglu_in"):
    m, k = x.shape
    tm = min(tm or MM_TM, m)
    tn = tn or 256
    assert m % tm == 0 and d_ff % tn == 0
    nb = d_ff // tn
    return pl.pallas_call(
        _mm_swiglu_kernel,
        grid=(m // tm, nb),
        in_specs=[pl.BlockSpec((tm, k), lambda i, j: (i, 0)),
                  pl.BlockSpec((k, tn), lambda i, j: (0, j)),
                  pl.BlockSpec((k, tn), lambda i, j: (0, nb + j))],
        out_specs=pl.BlockSpec((tm, tn), lambda i, j: (i, j)),
        out_shape=jax.ShapeDtypeStruct((m, d_ff), BF16),
        compiler_params=_cparams(2), name=name,
    )(x, w, w)


def _dot_f32(a, b):
    return jnp.dot(a, b, preferred_element_type=F32, precision=lax.Precision.HIGHEST)


def _gla_gate_kernel(h_ref, wa_ref, wg_ref, bg_ref, o_ref, *, chunk, rank):
    tm, d = h_ref.shape
    w_a = jnp.concatenate([wa_ref[...].astype(BF16), jnp.zeros((LANES - rank, d), BF16)], axis=0)
    a_low = lax.dot_general(h_ref[...], w_a, (((1,), (1,)), ((), ())),
                            preferred_element_type=F32)
    xg = _dot_f32(a_low, wg_ref[...]) + bg_ref[...]
    log_alpha = _log_sigmoid(xg) * (1.0 / GLA_GATE_TAU)
    row = lax.broadcasted_iota(jnp.int32, (tm, tm), 0)
    col = lax.broadcasted_iota(jnp.int32, (tm, tm), 1)
    same_chunk = (row // chunk) == (col // chunk)
    tril = jnp.where(same_chunk & (col <= row), 1.0, 0.0).astype(F32)
    o_ref[...] = _dot_f32(tril, log_alpha)


def _gla_gate(h, w_in_t, n_proj, w_gate, b_gate, chunk):
    m, d = h.shape
    rank, dk = w_gate.shape
    assert n_proj % rank == 0 and rank % 16 == 0 and rank <= LANES
    tm = min(256, m)
    wg_pad = jnp.zeros((LANES, dk), F32).at[:rank, :].set(w_gate)
    return pl.pallas_call(
        functools.partial(_gla_gate_kernel, chunk=chunk, rank=rank),
        grid=(m // tm,),
        in_specs=[pl.BlockSpec((tm, d), lambda i: (i, 0)),
                  pl.BlockSpec((rank, d), lambda i: (n_proj // rank, 0)),
                  pl.BlockSpec((LANES, dk), lambda i: (0, 0)),
                  pl.BlockSpec((1, dk), lambda i: (0, 0))],
        out_specs=pl.BlockSpec((tm, dk), lambda i: (i, 0)),
        out_shape=jax.ShapeDtypeStruct((m, dk), F32),
        compiler_params=_cparams(1), name="gla_gate",
    )(h, w_in_t, wg_pad, b_gate.reshape(1, dk))


def _gla_core_kernel(q_ref, k_ref, v_ref, r_ref, b_ref, gain_ref, o_ref, s_ref, *, scale, sub):
    chunk, hk = q_ref.shape

    @pl.when(pl.program_id(2) == 0)
    def _():
        s_ref[...] = jnp.zeros_like(s_ref)

    b = b_ref[...]
    q = q_ref[...].astype(F32) * scale
    k = k_ref[...].astype(F32)
    v = v_ref[...]
    state = s_ref[...]

    o = jnp.dot((q * jnp.exp(b)).astype(BF16), state.astype(BF16), preferred_element_type=F32)

    row_id = lax.broadcasted_iota(jnp.int32, (sub, chunk), 0)
    key_id = lax.broadcasted_iota(jnp.int32, (sub, chunk), 1)
    score_rows = []
    for i in range(chunk // sub):
        lo = i * sub
        b_i = b[lo:lo + sub, :]
        q_i = q[lo:lo + sub, :]
        diag = jnp.zeros((sub, chunk), F32)
        for j in range(sub):
            s = lo + j
            decay = jnp.exp(jnp.minimum(b_i - b[s:s + 1, :], 0.0))
            col = jnp.sum(q_i * decay * k[s:s + 1, :], axis=-1, keepdims=True)
            diag = jnp.where(key_id == s, col, diag)
        scores = jnp.where(key_id - lo <= row_id, diag, 0.0)
        if i > 0:
            b_first = b[lo:lo + 1, :]
            q_t = (q_i * jnp.exp(b_i - b_first)).astype(BF16)
            k_t = (k * jnp.exp(jnp.minimum(b_first - b, 0.0))).astype(BF16)
            below = lax.dot_general(q_t, k_t, (((1,), (1,)), ((), ())), preferred_element_type=F32)
            scores = jnp.where(key_id < lo, below, scores)
        score_rows.append(scores)
    scores = jnp.concatenate(score_rows, axis=0).astype(BF16)
    o = o + jnp.dot(scores, v, preferred_element_type=F32)

    b_last = b[chunk - 1:chunk, :]
    k_state = (k * jnp.exp(b_last - b)).astype(BF16)
    update = lax.dot_general(k_state, v, (((0,), (0,)), ((), ())), preferred_element_type=F32)
    decay_rows = jnp.broadcast_to(jnp.exp(b_last), (LANES, hk))
    decay_col = jnp.transpose(decay_rows)[:, 0:1]
    s_ref[...] = state * decay_col + update

    ms = jnp.mean(o * o, axis=-1, keepdims=True)
    o = o * lax.rsqrt(ms + RMS_EPS) * gain_ref[...]
    o_ref[...] = (o * _silu(r_ref[...].astype(F32))).astype(o_ref.dtype)


def _gla_core(proj, b, gain, batch, seq, heads, dk, dv):
    m = batch * seq
    hk, hv = dk // heads, dv // heads
    chunk = min(GLA_CHUNK, seq)
    nc = seq // chunk
    k_blk0 = dk // hk
    v_blk0 = (2 * dk) // hv
    r_blk0 = (2 * dk + dv) // hv

    def rows(bi, ci):
        return bi * nc + ci

    return pl.pallas_call(
        functools.partial(_gla_core_kernel, scale=float(hk) ** -0.5, sub=min(GLA_SUB, chunk)),
        grid=(batch, heads, nc),
        in_specs=[pl.BlockSpec((chunk, hk), lambda bi, h, c: (rows(bi, c), h)),
                  pl.BlockSpec((chunk, hk), lambda bi, h, c: (rows(bi, c), k_blk0 + h)),
                  pl.BlockSpec((chunk, hv), lambda bi, h, c: (rows(bi, c), v_blk0 + h)),
                  pl.BlockSpec((chunk, hv), lambda bi, h, c: (rows(bi, c), r_blk0 + h)),
                  pl.BlockSpec((chunk, hk), lambda bi, h, c: (rows(bi, c), h)),
                  pl.BlockSpec((1, hv), lambda bi, h, c: (0, 0))],
        out_specs=pl.BlockSpec((chunk, hv), lambda bi, h, c: (rows(bi, c), h)),
        out_shape=jax.ShapeDtypeStruct((m, dv), BF16),
        scratch_shapes=[pltpu.VMEM((hk, hv), F32)],
        compiler_params=_cparams(3), name="gla_core",
    )(proj, proj, proj, proj, b, gain.reshape(1, hv))


def _sb_kernel(q_ref, k_ref, v_ref, o_ref, acc_ref, carry_ref, hl0_ref, hl1_ref, lsp0_ref, lsp1_ref,
               *, scale, tk):
    tq, dh = q_ref.shape
    ratio = tq // tk
    assert ratio % 2 == 0
    hl_refs, lsp_refs = (hl0_ref, hl1_ref), (lsp0_ref, lsp1_ref)
    qi = pl.program_id(2)
    q = (q_ref[...].astype(F32) * (scale * LOG2_E)).astype(BF16)

    later = lax.broadcasted_iota(jnp.int32, (2 * tk, tk), 0) % tk
    key = lax.broadcasted_iota(jnp.int32, (2 * tk, tk), 1)
    suffix_ones = jnp.where(later > key, 1.0, 0.0).astype(BF16)

    acc_ref[...] = jnp.zeros_like(acc_ref)
    carry_ref[...] = jnp.zeros_like(carry_ref)

    def logit_stage(k_start, parity, strict):
        k_blk = k_ref[pl.ds(k_start, tk), :]
        z = lax.dot_general(q, k_blk, (((1,), (1,)), ((), ())), preferred_element_type=F32)
        neg_abs = pltpu.bitcast(pltpu.bitcast(z, jnp.uint32) | jnp.uint32(0x80000000), F32)
        ls_pos = jnp.minimum(z, 0.0) - jnp.log(1.0 + jnp.exp2(neg_abs)) * LOG2_E
        log_1m = ls_pos - z
        if strict is not None:
            log_1m = jnp.where(strict, log_1m, 0.0)
        hi = pltpu.bitcast(pltpu.bitcast(log_1m, jnp.uint32) & jnp.uint32(0xFFFF0000), F32)
        hl_refs[parity][:, :tk] = hi.astype(BF16)
        hl_refs[parity][:, tk:] = (log_1m - hi).astype(BF16)
        lsp_refs[parity][...] = ls_pos

    def value_stage(k_start, parity, strict):
        v_blk = v_ref[pl.ds(k_start, tk), :]
        hl = hl_refs[parity][...]
        suffix = jnp.dot(hl, suffix_ones, preferred_element_type=F32)
        w = jnp.exp2(lsp_refs[parity][...] + suffix + carry_ref[...])
        if strict is not None:
            w = jnp.where(strict, w, 0.0)
        acc_ref[...] += jnp.dot(w.astype(BF16), v_blk, preferred_element_type=F32)
        first = hl[:, 0:1].astype(F32) + hl[:, tk:tk + 1].astype(F32)
        carry_ref[...] += suffix[:, 0:1] + first

    def k_start_of(t):
        return pl.multiple_of(jnp.maximum(qi * tq + (ratio - 1 - t) * tk, 0), tk)

    row = lax.broadcasted_iota(jnp.int32, (tq, tk), 0)
    lane = lax.broadcasted_iota(jnp.int32, (tq, tk), 1)

    def strict_mask(d):
        return (lane + (ratio - 1 - d) * tk) < row

    logit_stage(k_start_of(0), 0, strict_mask(0))
    for d in range(ratio):
        logit_stage(k_start_of(d + 1), (d + 1) % 2, strict_mask(d + 1) if d + 1 < ratio else None)
        value_stage(k_start_of(d), d % 2, strict_mask(d))

    def body(it, _):
        for j in range(ratio):
            t = ratio + it * ratio + j
            logit_stage(k_start_of(t + 1), (j + 1) % 2, None)
            value_stage(k_start_of(t), j % 2, None)
        return 0

    lax.fori_loop(0, qi, body, 0)
    o_ref[...] = acc_ref[...].astype(o_ref.dtype)


def _sb_core(qkv, batch, seq, heads, d_model):
    dh = d_model // heads
    tq = min(SB_TQ, seq)
    tk = min(SB_TK, tq)
    nq = seq // tq
    return pl.pallas_call(
        functools.partial(_sb_kernel, scale=float(dh) ** -0.5, tk=tk),
        grid=(batch, heads, nq),
        in_specs=[pl.BlockSpec((tq, dh), lambda b, h, i: (b * nq + i, h)),
                  pl.BlockSpec((seq, dh), lambda b, h, i: (b, heads + h)),
                  pl.BlockSpec((seq, dh), lambda b, h, i: (b, 2 * heads + h))],
        out_specs=pl.BlockSpec((tq, dh), lambda b, h, i: (b * nq + i, h)),
        out_shape=jax.ShapeDtypeStruct((batch * seq, d_model), BF16),
        scratch_shapes=[pltpu.VMEM((tq, dh), F32), pltpu.VMEM((tq, 1), F32),
                        pltpu.VMEM((tq, 2 * tk), BF16), pltpu.VMEM((tq, 2 * tk), BF16),
                        pltpu.VMEM((tq, tk), F32), pltpu.VMEM((tq, tk), F32)],
        compiler_params=_cparams(3), name="sb_core",
    )(qkv, qkv, qkv)


def _router_kernel(x_ref, g_ref, wr_ref, h_ref, top_ref, *, n_experts):
    x = x_ref[...]
    ms = jnp.mean(x * x, axis=-1, keepdims=True)
    h = x * lax.rsqrt(ms + RMS_EPS) * g_ref[...]
    h_ref[...] = h
    logits = _dot_f32(h, wr_ref[...])
    lane = lax.broadcasted_iota(jnp.int32, logits.shape, 1)
    logits = jnp.where(lane < n_experts, logits, -jnp.inf)
    v1 = jnp.max(logits, axis=-1, keepdims=True)
    i1 = jnp.min(jnp.where(logits == v1, lane, LANES), axis=-1, keepdims=True)
    rest = jnp.where(lane == i1, -jnp.inf, logits)
    v2 = jnp.max(rest, axis=-1, keepdims=True)
    i2 = jnp.min(jnp.where(rest == v2, lane, LANES), axis=-1, keepdims=True)
    e = jnp.exp(v2 - v1)
    g1 = 1.0 / (1.0 + e)
    g2 = e * g1
    out = jnp.where(lane == 0, i1.astype(F32),
                    jnp.where(lane == 1, i2.astype(F32),
                              jnp.where(lane == 2, g1, jnp.where(lane == 3, g2, 0.0))))
    top_ref[...] = out


def _router(x, g, w_router):
    m, d = x.shape
    n_experts = w_router.shape[1]
    tm = min(256, m)
    wr_pad = jnp.zeros((d, LANES), F32).at[:, :n_experts].set(w_router)
    return pl.pallas_call(
        functools.partial(_router_kernel, n_experts=n_experts),
        grid=(m // tm,),
        in_specs=[pl.BlockSpec((tm, d), lambda i: (i, 0)),
                  pl.BlockSpec((1, d), lambda i: (0, 0)),
                  pl.BlockSpec((d, LANES), lambda i: (0, 0))],
        out_specs=[pl.BlockSpec((tm, d), lambda i: (i, 0)),
                   pl.BlockSpec((tm, LANES), lambda i: (i, 0))],
        out_shape=[jax.ShapeDtypeStruct((m, d), F32),
                   jax.ShapeDtypeStruct((m, LANES), F32)],
        compiler_params=_cparams(1), name="moe_router",
    )(x, g.reshape(1, d), wr_pad)


def _row_copy(src_hbm, dst_ref, sem, src_row, dst_row):
    return pltpu.make_async_copy(src_hbm.at[pl.ds(src_row, 1), :],
                                 dst_ref.at[pl.ds(dst_row, 1), :], sem)


def _gather_kernel(idx_ref, nrows_ref, src_hbm, o_ref, buf_ref, sem):
    rows = o_ref.shape[0]
    base = pl.program_id(0) * rows

    @pl.when(base < nrows_ref[0])
    def _():
        def start(r, _):
            _row_copy(src_hbm, buf_ref, sem, idx_ref[base + r], r).start()
            return 0

        def wait(r, _):
            _row_copy(src_hbm, buf_ref, sem, 0, r).wait()
            return 0

        lax.fori_loop(0, rows, start, 0)
        lax.fori_loop(0, rows, wait, 0)
        o_ref[...] = buf_ref[...].astype(o_ref.dtype)

    @pl.when(base >= nrows_ref[0])
    def _():
        o_ref[...] = jnp.zeros_like(o_ref)


def _gather_rows(src, idx, n_rows_valid, out_dtype):
    n = idx.shape[0]
    d = src.shape[1]
    rows = min(GATHER_ROWS, n)
    assert n % rows == 0
    return pl.pallas_call(
        _gather_kernel,
        grid_spec=pltpu.PrefetchScalarGridSpec(
            num_scalar_prefetch=2, grid=(n // rows,),
            in_specs=[pl.BlockSpec(memory_space=pl.ANY)],
            out_specs=pl.BlockSpec((rows, d), lambda i, idx, nr: (i, 0)),
            scratch_shapes=[pltpu.VMEM((rows, d), src.dtype), pltpu.SemaphoreType.DMA(())]),
        out_shape=jax.ShapeDtypeStruct((n, d), out_dtype),
        compiler_params=_cparams(1), name="moe_gather",
    )(idx, n_rows_valid, src)


def _expert_changes(eid_ref, i):
    return (i == 0) | (eid_ref[i] != eid_ref[jnp.maximum(i - 1, 0)])


def _moe_in_kernel(eid_ref, nvalid_ref, x_ref, wg_ref, wu_ref, o_ref, wb_ref, *, last_shift):
    j, i = pl.program_id(0), pl.program_id(1)
    tn = o_ref.shape[1]

    @pl.when(_expert_changes(eid_ref, i))
    def _():
        wb_ref[:, :tn] = wg_ref[...].astype(BF16)
        wb_ref[:, tn:] = wu_ref[...].astype(BF16)

    @pl.when(i < nvalid_ref[0])
    def _():
        gu = jnp.dot(x_ref[...], wb_ref[...], preferred_element_type=F32)
        act = (_silu(gu[:, :tn]) * gu[:, tn:]).astype(o_ref.dtype)
        if last_shift == 0:
            o_ref[...] = act
        else:
            is_last = j == pl.num_programs(0) - 1

            @pl.when(is_last)
            def _():
                o_ref[:, :tn - last_shift] = act[:, last_shift:]
                o_ref[:, tn - last_shift:] = jnp.zeros((act.shape[0], last_shift), o_ref.dtype)

            @pl.when(jnp.logical_not(is_last))
            def _():
                o_ref[...] = act

    @pl.when(i >= nvalid_ref[0])
    def _():
        o_ref[...] = jnp.zeros_like(o_ref)


def _moe_in(xs, w_in, tile_expert, n_valid, tm):
    r, d = xs.shape
    d_ff = w_in.shape[2] // 2
    tn = min(MOE_IN_TN, d_ff)
    nb = pl.cdiv(d_ff, tn)

    assert d_ff % LANES == 0 and tn % LANES == 0

    def col(j, base=0):
        return (jnp.minimum(j * (tn // LANES), (d_ff - tn) // LANES) + base // LANES) * LANES

    def x_map(j, i, eid, nv):
        return (jnp.minimum(i, nv[0] - 1), 0)

    return pl.pallas_call(
        functools.partial(_moe_in_kernel, last_shift=nb * tn - d_ff),
        grid_spec=pltpu.PrefetchScalarGridSpec(
            num_scalar_prefetch=2, grid=(nb, r // tm),
            in_specs=[pl.BlockSpec((tm, d), x_map),
                      pl.BlockSpec((None, pl.Element(d), pl.Element(tn)),
                                   lambda j, i, eid, nv: (eid[i], 0, col(j))),
                      pl.BlockSpec((None, pl.Element(d), pl.Element(tn)),
                                   lambda j, i, eid, nv: (eid[i], 0, col(j, d_ff)))],
            out_specs=pl.BlockSpec((tm, tn), lambda j, i, eid, nv: (i, j)),
            scratch_shapes=[pltpu.VMEM((d, 2 * tn), BF16)]),
        out_shape=jax.ShapeDtypeStruct((r, nb * tn), BF16),
        compiler_params=_cparams(2), name="moe_in",
    )(tile_expert, n_valid, xs, w_in, w_in)


def _moe_out_kernel(eid_ref, nvalid_ref, x_ref, w_ref, s_ref, o_ref, wb_ref):
    i = pl.program_id(1)

    @pl.when(_expert_changes(eid_ref, i))
    def _():
        wb_ref[...] = w_ref[...].astype(BF16)

    @pl.when(i < nvalid_ref[0])
    def _():
        acc = jnp.dot(x_ref[...], wb_ref[...], preferred_element_type=F32)
        o_ref[...] = acc * s_ref[...]

    @pl.when(i >= nvalid_ref[0])
    def _():
        o_ref[...] = jnp.zeros_like(o_ref)


def _moe_out(act, w_out, row_gate, tile_expert, n_valid, tm):
    r = act.shape[0]
    d_ff, d = w_out.shape[1:]
    tn = min(MOE_OUT_TN, d)

    def x_map(j, i, eid, nv):
        return (jnp.minimum(i, nv[0] - 1), 0)

    return pl.pallas_call(
        _moe_out_kernel,
        grid_spec=pltpu.PrefetchScalarGridSpec(
            num_scalar_prefetch=2, grid=(d // tn, r // tm),
            in_specs=[pl.BlockSpec((tm, d_ff), x_map),
                      pl.BlockSpec((None, d_ff, tn), lambda j, i, eid, nv: (eid[i], 0, j)),
                      pl.BlockSpec((tm, 1), x_map)],
            out_specs=pl.BlockSpec((tm, tn), lambda j, i, eid, nv: (i, j)),
            scratch_shapes=[pltpu.VMEM((d_ff, tn), BF16)]),
        out_shape=jax.ShapeDtypeStruct((r, d), F32),
        compiler_params=_cparams(2), name="moe_out",
    )(tile_expert, n_valid, act, w_out, row_gate.reshape(r, 1))


def _combine_kernel(dest_ref, x_ref, y_hbm, g_ref, o_ref, a_ref, b_ref, sem):
    rows = x_ref.shape[0]
    base = pl.program_id(0) * rows

    def start(r, _):
        _row_copy(y_hbm, a_ref, sem, dest_ref[2 * (base + r)], r).start()
        _row_copy(y_hbm, b_ref, sem, dest_ref[2 * (base + r) + 1], r).start()
        return 0

    def wait(r, _):
        _row_copy(y_hbm, a_ref, sem, 0, r).wait()
        _row_copy(y_hbm, b_ref, sem, 0, r).wait()
        return 0

    lax.fori_loop(0, rows, start, 0)
    lax.fori_loop(0, rows, wait, 0)
    x = x_ref[...] + (a_ref[...] + b_ref[...])
    ms = jnp.mean(x * x, axis=-1, keepdims=True)
    o_ref[...] = x * lax.rsqrt(ms + RMS_EPS) * g_ref[...]


def _combine_norm(x, y, dest, g):
    m, d = x.shape
    rows = min(COMBINE_ROWS, m)
    return pl.pallas_call(
        _combine_kernel,
        grid_spec=pltpu.PrefetchScalarGridSpec(
            num_scalar_prefetch=1, grid=(m // rows,),
            in_specs=[pl.BlockSpec((rows, d), lambda i, dest: (i, 0)),
                      pl.BlockSpec(memory_space=pl.ANY),
                      pl.BlockSpec((1, d), lambda i, dest: (0, 0))],
            out_specs=pl.BlockSpec((rows, d), lambda i, dest: (i, 0)),
            scratch_shapes=[pltpu.VMEM((rows, d), F32), pltpu.VMEM((rows, d), F32),
                            pltpu.SemaphoreType.DMA(())]),
        out_shape=jax.ShapeDtypeStruct((m, d), F32),
        compiler_params=_cparams(1), name="moe_combine_norm",
    )(dest, x, y, g.reshape(1, d))


def _dispatch_plan(top, n_experts, tm):
    m = top.shape[0]
    n_pairs = m * TOP_K
    expert = top[:, :TOP_K].astype(jnp.int32).reshape(n_pairs)
    gate = top[:, TOP_K:2 * TOP_K].reshape(n_pairs)
    onehot = (expert[:, None] == jnp.arange(n_experts, dtype=jnp.int32)[None, :]).astype(jnp.int32)
    before = jnp.cumsum(onehot, axis=0) - onehot
    rank = jnp.sum(before * onehot, axis=1)
    counts = jnp.sum(onehot, axis=0)
    tiles = (counts + tm - 1) // tm
    tile_end = jnp.cumsum(tiles)
    group_start = (tile_end - tiles) * tm
    dest = group_start[expert] + rank
    n_tiles = n_pairs // tm + n_experts
    n_rows = n_tiles * tm
    src_token = jnp.zeros((n_rows,), jnp.int32).at[dest].set(jnp.arange(n_pairs, dtype=jnp.int32) // TOP_K)
    row_gate = jnp.zeros((n_rows,), F32).at[dest].set(gate)
    tile_id = jnp.arange(n_tiles, dtype=jnp.int32)
    tile_expert = jnp.minimum(jnp.sum((tile_end[None, :] <= tile_id[:, None]).astype(jnp.int32), axis=1),
                              n_experts - 1)
    n_valid = tile_end[-1:].astype(jnp.int32)
    return dest.astype(jnp.int32), src_token, row_gate, tile_expert, n_valid


def kernel(x, attn_norm, ffn_norm, gla_w_in, gla_w_gate, gla_b_gate, gla_onorm, gla_w_out,
           sb_w_in, sb_w_out, dense_w_in, dense_w_out, moe_router, moe_w_in, moe_w_out,
           final_norm):
    batch, seq, d = x.shape
    m = batch * seq
    x = x.reshape(m, d)

    rank, dk = gla_w_gate.shape[1:]
    hv = gla_onorm.shape[1]
    dv = GLA_HEADS * hv
    n_proj = 2 * dk + 2 * dv
    h = _rmsnorm(x, attn_norm[0], BF16)
    w_in_t = jnp.transpose(gla_w_in[0])
    proj = _matmul_nt(h, w_in_t, n_cols=n_proj, name="gla_in")
    b = _gla_gate(h, w_in_t, n_proj, gla_w_gate[0], gla_b_gate[0], min(GLA_CHUNK, seq))
    o = _gla_core(proj, b, gla_onorm[0], batch, seq, GLA_HEADS, dk, dv)
    x = _matmul(o, gla_w_out[0], n_cols=d, tk=dv, res=x, name="gla_out")

    d_ff = dense_w_out.shape[1]
    h = _rmsnorm(x, ffn_norm[0], BF16)
    act = _swiglu_in(h, dense_w_in[0], d_ff, name="dense_in")
    half = d_ff // 2
    x = _matmul(act, dense_w_out[0], n_cols=d, tk=half, k_blk=0, tn=256, res=x, name="dense_out0")
    x = _matmul(act, dense_w_out[0], n_cols=d, tk=half, k_blk=1, tn=256, res=x, name="dense_out1")

    h = _rmsnorm(x, attn_norm[1], BF16)
    qkv = _matmul(h, sb_w_in[0], n_cols=3 * d, tk=d, name="sb_in")
    o = _sb_core(qkv, batch, seq, SB_HEADS, d)
    x = _matmul(o, sb_w_out[0], n_cols=d, tk=d, res=x, name="sb_out")

    n_experts = moe_router.shape[2]
    tm = min(MOE_TM, m)
    h32, top = _router(x, ffn_norm[1], moe_router[0])
    dest, src_token, row_gate, tile_expert, n_valid = _dispatch_plan(top, n_experts, tm)
    xs = _gather_rows(h32, src_token, n_valid * tm, BF16)
    act = _moe_in(xs, moe_w_in[0], tile_expert, n_valid, tm)
    y = _moe_out(act, moe_w_out[0], row_gate, tile_expert, n_valid, tm)
    out = _combine_norm(x, y, dest, final_norm)
    return out.reshape(batch, seq, d)
```

```python
import functools

import jax
import jax.numpy as jnp
from jax import lax
from jax.experimental import pallas as pl
from jax.experimental.pallas import tpu as pltpu

F32 = jnp.float32
BF16 = jnp.bfloat16

RMS_EPS = 1e-6
LOG2_E = 1.4426950408889634
GLA_HEADS = 4
GLA_GATE_TAU = 16.0
GLA_CHUNK = 64
GLA_SUB = 16
SB_HEADS = 32
SB_TQ = 512
SB_TK = 256
TOP_K = 2
LANES = 128
VMEM_LIMIT_BYTES = 56 * 1024 * 1024

MM_TM = 1024
MM_TN = 512
MOE_TM = 512
MOE_IN_TN = 512
MOE_OUT_TN = 512
GATHER_ROWS = 512
COMBINE_ROWS = 256


def _cparams(n_axes):
    return pltpu.CompilerParams(
        dimension_semantics=("arbitrary",) * n_axes,
        vmem_limit_bytes=VMEM_LIMIT_BYTES)


def _silu(x):
    return x / (1.0 + jnp.exp(-x))


def _log_sigmoid(x):
    return jnp.minimum(x, 0.0) - jnp.log1p(jnp.exp(-jnp.abs(x)))


def _rmsnorm_kernel(x_ref, g_ref, o_ref):
    x = x_ref[...]
    ms = jnp.mean(x * x, axis=-1, keepdims=True)
    o_ref[...] = (x * lax.rsqrt(ms + RMS_EPS) * g_ref[...]).astype(o_ref.dtype)


def _rmsnorm(x, g, out_dtype):
    m, d = x.shape
    tm = min(256, m)
    return pl.pallas_call(
        _rmsnorm_kernel,
        grid=(m // tm,),
        in_specs=[pl.BlockSpec((tm, d), lambda i: (i, 0)),
                  pl.BlockSpec((1, d), lambda i: (0, 0))],
        out_specs=pl.BlockSpec((tm, d), lambda i: (i, 0)),
        out_shape=jax.ShapeDtypeStruct((m, d), out_dtype),
        compiler_params=_cparams(1),
        name="rmsnorm",
    )(x, g.reshape(1, d))


def _mm_plain_kernel(x_ref, w_ref, o_ref):
    acc = jnp.dot(x_ref[...], w_ref[...].astype(BF16), preferred_element_type=F32)
    o_ref[...] = acc.astype(o_ref.dtype)


def _mm_nt_kernel(x_ref, wt_ref, o_ref):
    acc = lax.dot_general(x_ref[...], wt_ref[...].astype(BF16), (((1,), (1,)), ((), ())),
                          preferred_element_type=F32)
    o_ref[...] = acc.astype(o_ref.dtype)


def _matmul_nt(x, wt, *, n_cols, tm=None, tn=None, out_dtype=BF16, name="matmul_nt"):
    m, k = x.shape
    tm = min(tm or MM_TM, m)
    tn = min(tn or MM_TN, n_cols)
    assert m % tm == 0 and n_cols % tn == 0
    return pl.pallas_call(
        _mm_nt_kernel, grid=(m // tm, n_cols // tn),
        in_specs=[pl.BlockSpec((tm, k), lambda i, j: (i, 0)),
                  pl.BlockSpec((tn, k), lambda i, j: (j, 0))],
        out_specs=pl.BlockSpec((tm, tn), lambda i, j: (i, j)),
        out_shape=jax.ShapeDtypeStruct((m, n_cols), out_dtype),
        compiler_params=_cparams(2), name=name,
    )(x, wt)


def _mm_res_kernel(x_ref, w_ref, res_ref, o_ref):
    acc = jnp.dot(x_ref[...], w_ref[...].astype(BF16), preferred_element_type=F32)
    o_ref[...] = res_ref[...] + acc


def _mm_swiglu_kernel(x_ref, wg_ref, wu_ref, o_ref):
    x = x_ref[...]
    g = jnp.dot(x, wg_ref[...].astype(BF16), preferred_element_type=F32)
    u = jnp.dot(x, wu_ref[...].astype(BF16), preferred_element_type=F32)
    o_ref[...] = (_silu(g) * u).astype(o_ref.dtype)


def _matmul(x, w, *, n_cols, tk, k_blk=0, w_col_blk=0, tm=None, tn=None,
            out_dtype=BF16, res=None, name="matmul"):
    m = x.shape[0]
    tm = min(tm or MM_TM, m)
    tn = min(tn or MM_TN, n_cols)
    assert m % tm == 0 and n_cols % tn == 0
    grid = (m // tm, n_cols // tn)
    x_spec = pl.BlockSpec((tm, tk), lambda i, j: (i, k_blk))
    w_spec = pl.BlockSpec((tk, tn), lambda i, j: (k_blk, w_col_blk + j))
    o_spec = pl.BlockSpec((tm, tn), lambda i, j: (i, j))
    if res is None:
        kern, in_specs, args = _mm_plain_kernel, [x_spec, w_spec], (x, w)
    else:
        kern, in_specs, args = _mm_res_kernel, [x_spec, w_spec, o_spec], (x, w, res)
        out_dtype = F32
    return pl.pallas_call(
        kern, grid=grid, in_specs=in_specs, out_specs=o_spec,
        out_shape=jax.ShapeDtypeStruct((m, n_cols), out_dtype),
        compiler_params=_cparams(2), name=name,
    )(*args)


def _swiglu_in(x, w, d_ff, *, tm=None, tn=None, name="swiglu_in"):
    m, k = x.shape
    tm = min(tm or MM_TM, m)
    tn = tn or 256
    assert m % tm == 0 and d_ff % tn == 0
    nb = d_ff // tn
    return pl.pallas_call(
        _mm_swiglu_kernel,
        grid=(m // tm, nb),
        in_specs=[pl.BlockSpec((tm, k), lambda i, j: (i, 0)),
                  pl.BlockSpec((k, tn), lambda i, j: (0, j)),
                  pl.BlockSpec((k, tn), lambda i, j: (0, nb + j))],
        out_specs=pl.BlockSpec((tm, tn), lambda i, j: (i, j)),
        out_shape=jax.ShapeDtypeStruct((m, d_ff), BF16),
        compiler_params=_cparams(2), name=name,
    )(x, w, w)


def _dot_f32(a, b):
    return jnp.dot(a, b, preferred_element_type=F32, precision=lax.Precision.HIGHEST)


def _gla_gate_kernel(h_ref, wa_ref, wg_ref, bg_ref, o_ref, *, chunk, rank):
    tm, d = h_ref.shape
    w_a = jnp.concatenate([wa_ref[...].astype(BF16), jnp.zeros((LANES - rank, d), BF16)], axis=0)
    a_low = lax.dot_general(h_ref[...], w_a, (((1,), (1,)), ((), ())),
                            preferred_element_type=F32)
    xg = _dot_f32(a_low, wg_ref[...]) + bg_ref[...]
    log_alpha = _log_sigmoid(xg) * (1.0 / GLA_GATE_TAU)
    row = lax.broadcasted_iota(jnp.int32, (tm, tm), 0)
    col = lax.broadcasted_iota(jnp.int32, (tm, tm), 1)
    same_chunk = (row // chunk) == (col // chunk)
    tril = jnp.where(same_chunk & (col <= row), 1.0, 0.0).astype(F32)
    o_ref[...] = _dot_f32(tril, log_alpha)


def _gla_gate(h, w_in_t, n_proj, w_gate, b_gate, chunk):
    m, d = h.shape
    rank, dk = w_gate.shape
    assert n_proj % rank == 0 and rank % 16 == 0 and rank <= LANES
    tm = min(256, m)
    wg_pad = jnp.zeros((LANES, dk), F32).at[:rank, :].set(w_gate)
    return pl.pallas_call(
        functools.partial(_gla_gate_kernel, chunk=chunk, rank=rank),
        grid=(m // tm,),
        in_specs=[pl.BlockSpec((tm, d), lambda i: (i, 0)),
                  pl.BlockSpec((rank, d), lambda i: (n_proj // rank, 0)),
                  pl.BlockSpec((LANES, dk), lambda i: (0, 0)),
                  pl.BlockSpec((1, dk), lambda i: (0, 0))],
        out_specs=pl.BlockSpec((tm, dk), lambda i: (i, 0)),
        out_shape=jax.ShapeDtypeStruct((m, dk), F32),
        compiler_params=_cparams(1), name="gla_gate",
    )(h, w_in_t, wg_pad, b_gate.reshape(1, dk))


def _gla_core_kernel(q_ref, k_ref, v_ref, r_ref, b_ref, gain_ref, o_ref, s_ref, *, scale, sub):
    chunk, hk = q_ref.shape

    @pl.when(pl.program_id(2) == 0)
    def _():
        s_ref[...] = jnp.zeros_like(s_ref)

    b = b_ref[...]
    q = q_ref[...].astype(F32) * scale
    k = k_ref[...].astype(F32)
    v = v_ref[...]
    state = s_ref[...]

    o = jnp.dot((q * jnp.exp(b)).astype(BF16), state.astype(BF16), preferred_element_type=F32)

    row_id = lax.broadcasted_iota(jnp.int32, (sub, chunk), 0)
    key_id = lax.broadcasted_iota(jnp.int32, (sub, chunk), 1)
    score_rows = []
    for i in range(chunk // sub):
        lo = i * sub
        b_i = b[lo:lo + sub, :]
        q_i = q[lo:lo + sub, :]
        diag = jnp.zeros((sub, chunk), F32)
        for j in range(sub):
            s = lo + j
            decay = jnp.exp(jnp.minimum(b_i - b[s:s + 1, :], 0.0))
            col = jnp.sum(q_i * decay * k[s:s + 1, :], axis=-1, keepdims=True)
            diag = jnp.where(key_id == s, col, diag)
        scores = jnp.where(key_id - lo <= row_id, diag, 0.0)
        if i > 0:
            b_first = b[lo:lo + 1, :]
            q_t = (q_i * jnp.exp(b_i - b_first)).astype(BF16)
            k_t = (k * jnp.exp(jnp.minimum(b_first - b, 0.0))).astype(BF16)
            below = lax.dot_general(q_t, k_t, (((1,), (1,)), ((), ())), preferred_element_type=F32)
            scores = jnp.where(key_id < lo, below, scores)
        score_rows.append(scores)
    scores = jnp.concatenate(score_rows, axis=0).astype(BF16)
    o = o + jnp.dot(scores, v, preferred_element_type=F32)

    b_last = b[chunk - 1:chunk, :]
    k_state = (k * jnp.exp(b_last - b)).astype(BF16)
    update = lax.dot_general(k_state, v, (((0,), (0,)), ((), ())), preferred_element_type=F32)
    decay_rows = jnp.broadcast_to(jnp.exp(b_last), (LANES, hk))
    decay_col = jnp.transpose(decay_rows)[:, 0:1]
    s_ref[...] = state * decay_col + update

    ms = jnp.mean(o * o, axis=-1, keepdims=True)
    o = o * lax.rsqrt(ms + RMS_EPS) * gain_ref[...]
    o_ref[...] = (o * _silu(r_ref[...].astype(F32))).astype(o_ref.dtype)


def _gla_core(proj, b, gain, batch, seq, heads, dk, dv):
    m = batch * seq
    hk, hv = dk // heads, dv // heads
    chunk = min(GLA_CHUNK, seq)
    nc = seq // chunk
    k_blk0 = dk // hk
    v_blk0 = (2 * dk) // hv
    r_blk0 = (2 * dk + dv) // hv

    def rows(bi, ci):
        return bi * nc + ci

    return pl.pallas_call(
        functools.partial(_gla_core_kernel, scale=float(hk) ** -0.5, sub=min(GLA_SUB, chunk)),
        grid=(batch, heads, nc),
        in_specs=[pl.BlockSpec((chunk, hk), lambda bi, h, c: (rows(bi, c), h)),
                  pl.BlockSpec((chunk, hk), lambda bi, h, c: (rows(bi, c), k_blk0 + h)),
                  pl.BlockSpec((chunk, hv), lambda bi, h, c: (rows(bi, c), v_blk0 + h)),
                  pl.BlockSpec((chunk, hv), lambda bi, h, c: (rows(bi, c), r_blk0 + h)),
                  pl.BlockSpec((chunk, hk), lambda bi, h, c: (rows(bi, c), h)),
                  pl.BlockSpec((1, hv), lambda bi, h, c: (0, 0))],
        out_specs=pl.BlockSpec((chunk, hv), lambda bi, h, c: (rows(bi, c), h)),
        out_shape=jax.ShapeDtypeStruct((m, dv), BF16),
        scratch_shapes=[pltpu.VMEM((hk, hv), F32)],
        compiler_params=_cparams(3), name="gla_core",
    )(proj, proj, proj, proj, b, gain.reshape(1, hv))


def _sb_kernel(q_ref, k_ref, v_ref, o_ref, acc_ref, carry_ref, hl0_ref, hl1_ref, lsp0_ref, lsp1_ref,
               sl0_ref, sl1_ref, tot0_ref, tot1_ref, *, scale, tk):
    tq, dh = q_ref.shape
    ratio = tq // tk
    assert ratio % 2 == 0
    hl_refs, lsp_refs = (hl0_ref, hl1_ref), (lsp0_ref, lsp1_ref)
    sl_refs, tot_refs = (sl0_ref, sl1_ref), (tot0_ref, tot1_ref)
    qi = pl.program_id(2)
    q = (q_ref[...].astype(F32) * (scale * LOG2_E)).astype(BF16)

    later = lax.broadcasted_iota(jnp.int32, (tk, tk), 0)
    key = lax.broadcasted_iota(jnp.int32, (tk, tk), 1)
    suffix_ones = jnp.where((later > key) | (key == tk - 1), 1.0, 0.0).astype(BF16)

    acc_ref[...] = jnp.zeros_like(acc_ref)
    carry_ref[...] = jnp.zeros_like(carry_ref)

    def logit_stage(k_start, parity, strict):
        k_blk = k_ref[pl.ds(k_start, tk), :]
        z = lax.dot_general(q, k_blk, (((1,), (1,)), ((), ())), preferred_element_type=F32)
        neg_abs = pltpu.bitcast(pltpu.bitcast(z, jnp.uint32) | jnp.uint32(0x80000000), F32)
        ls_pos = jnp.minimum(z, 0.0) - jnp.log(1.0 + jnp.exp2(neg_abs)) * LOG2_E
        log_1m = ls_pos - z
        if strict is not None:
            log_1m = jnp.where(strict, log_1m, 0.0)
        hl_refs[parity][...] = log_1m.astype(BF16)
        lsp_refs[parity][...] = ls_pos

    def suffix_stage(parity):
        sums = jnp.dot(hl_refs[parity][...], suffix_ones, preferred_element_type=F32)
        sl_refs[parity][...] = lsp_refs[parity][...] + jnp.where(lane == tk - 1, 0.0, sums)
        tot_refs[parity][...] = jnp.broadcast_to(sums[:, tk - 1:tk], (tq, LANES))

    def value_stage(k_start, parity, strict):
        v_blk = v_ref[pl.ds(k_start, tk), :]
        carry = carry_ref[...]
        w = jnp.exp2(sl_refs[parity][...] + jnp.concatenate([carry] * (tk // LANES), axis=1))
        if strict is not None:
            w = jnp.where(strict, w, 0.0)
        acc_ref[...] += jnp.dot(w.astype(BF16), v_blk, preferred_element_type=F32)
        carry_ref[...] = carry + tot_refs[parity][...]

    def k_start_of(t):
        return pl.multiple_of(jnp.maximum(qi * tq + (ratio - 1 - t) * tk, 0), tk)

    row = lax.broadcasted_iota(jnp.int32, (tq, tk), 0)
    lane = lax.broadcasted_iota(jnp.int32, (tq, tk), 1)

    def strict_mask(t):
        return (lane + (ratio - 1 - t) * tk) < row if t < ratio else None

    def step(s, parity, mask_s, mask_ahead):
        logit_stage(k_start_of(s + 2), parity, mask_ahead)
        suffix_stage(1 - parity)
        value_stage(k_start_of(s), parity, mask_s)

    logit_stage(k_start_of(0), 0, strict_mask(0))
    logit_stage(k_start_of(1), 1, strict_mask(1))
    suffix_stage(0)
    for s in range(ratio):
        step(s, s % 2, strict_mask(s), strict_mask(s + 2))

    def body(it, _):
        for j in range(ratio):
            step(ratio + it * ratio + j, j % 2, None, None)
        return 0

    lax.fori_loop(0, qi, body, 0)
    o_ref[...] = acc_ref[...].astype(o_ref.dtype)


def _sb_core(qkv, batch, seq, heads, d_model):
    dh = d_model // heads
    tq = min(SB_TQ, seq)
    tk = min(SB_TK, tq)
    nq = seq // tq
    return pl.pallas_call(
        functools.partial(_sb_kernel, scale=float(dh) ** -0.5, tk=tk),
        grid=(batch, heads, nq),
        in_specs=[pl.BlockSpec((tq, dh), lambda b, h, i: (b * nq + i, h)),
                  pl.BlockSpec((seq, dh), lambda b, h, i: (b, heads + h)),
                  pl.BlockSpec((seq, dh), lambda b, h, i: (b, 2 * heads + h))],
        out_specs=pl.BlockSpec((tq, dh), lambda b, h, i: (b * nq + i, h)),
        out_shape=jax.ShapeDtypeStruct((batch * seq, d_model), BF16),
        scratch_shapes=[pltpu.VMEM((tq, dh), F32), pltpu.VMEM((tq, LANES), F32),
                        pltpu.VMEM((tq, tk), BF16), pltpu.VMEM((tq, tk), BF16),
                        pltpu.VMEM((tq, tk), F32), pltpu.VMEM((tq, tk), F32),
                        pltpu.VMEM((tq, tk), F32), pltpu.VMEM((tq, tk), F32),
                        pltpu.VMEM((tq, LANES), F32), pltpu.VMEM((tq, LANES), F32)],
        compiler_params=_cparams(3), name="sb_core",
    )(qkv, qkv, qkv)


def _router_kernel(x_ref, g_ref, wr_ref, h_ref, top_ref, *, n_experts):
    x = x_ref[...]
    ms = jnp.mean(x * x, axis=-1, keepdims=True)
    h = x * lax.rsqrt(ms + RMS_EPS) * g_ref[...]
    h_ref[...] = h
    logits = _dot_f32(h, wr_ref[...])
    lane = lax.broadcasted_iota(jnp.int32, logits.shape, 1)
    logits = jnp.where(lane < n_experts, logits, -jnp.inf)
    v1 = jnp.max(logits, axis=-1, keepdims=True)
    i1 = jnp.min(jnp.where(logits == v1, lane, LANES), axis=-1, keepdims=True)
    rest = jnp.where(lane == i1, -jnp.inf, logits)
    v2 = jnp.max(rest, axis=-1, keepdims=True)
    i2 = jnp.min(jnp.where(rest == v2, lane, LANES), axis=-1, keepdims=True)
    e = jnp.exp(v2 - v1)
    g1 = 1.0 / (1.0 + e)
    g2 = e * g1
    out = jnp.where(lane == 0, i1.astype(F32),
                    jnp.where(lane == 1, i2.astype(F32),
                              jnp.where(lane == 2, g1, jnp.where(lane == 3, g2, 0.0))))
    top_ref[...] = out


def _router(x, g, w_router):
    m, d = x.shape
    n_experts = w_router.shape[1]
    tm = min(256, m)
    wr_pad = jnp.zeros((d, LANES), F32).at[:, :n_experts].set(w_router)
    return pl.pallas_call(
        functools.partial(_router_kernel, n_experts=n_experts),
        grid=(m // tm,),
        in_specs=[pl.BlockSpec((tm, d), lambda i: (i, 0)),
                  pl.BlockSpec((1, d), lambda i: (0, 0)),
                  pl.BlockSpec((d, LANES), lambda i: (0, 0))],
        out_specs=[pl.BlockSpec((tm, d), lambda i: (i, 0)),
                   pl.BlockSpec((tm, LANES), lambda i: (i, 0))],
        out_shape=[jax.ShapeDtypeStruct((m, d), F32),
                   jax.ShapeDtypeStruct((m, LANES), F32)],
        compiler_params=_cparams(1), name="moe_router",
    )(x, g.reshape(1, d), wr_pad)


def _row_copy(src_hbm, dst_ref, sem, src_row, dst_row):
    return pltpu.make_async_copy(src_hbm.at[pl.ds(src_row, 1), :],
                                 dst_ref.at[pl.ds(dst_row, 1), :], sem)


def _gather_kernel(idx_ref, nrows_ref, src_hbm, o_ref, buf_ref, sem):
    rows = o_ref.shape[0]
    base = pl.program_id(0) * rows

    @pl.when(base < nrows_ref[0])
    def _():
        def start(r, _):
            _row_copy(src_hbm, buf_ref, sem, idx_ref[base + r], r).start()
            return 0

        def wait(r, _):
            _row_copy(src_hbm, buf_ref, sem, 0, r).wait()
            return 0

        lax.fori_loop(0, rows, start, 0)
        lax.fori_loop(0, rows, wait, 0)
        o_ref[...] = buf_ref[...].astype(o_ref.dtype)

    @pl.when(base >= nrows_ref[0])
    def _():
        o_ref[...] = jnp.zeros_like(o_ref)


def _gather_rows(src, idx, n_rows_valid, out_dtype):
    n = idx.shape[0]
    d = src.shape[1]
    rows = min(GATHER_ROWS, n)
    assert n % rows == 0
    return pl.pallas_call(
        _gather_kernel,
        grid_spec=pltpu.PrefetchScalarGridSpec(
            num_scalar_prefetch=2, grid=(n // rows,),
            in_specs=[pl.BlockSpec(memory_space=pl.ANY)],
            out_specs=pl.BlockSpec((rows, d), lambda i, idx, nr: (i, 0)),
            scratch_shapes=[pltpu.VMEM((rows, d), src.dtype), pltpu.SemaphoreType.DMA(())]),
        out_shape=jax.ShapeDtypeStruct((n, d), out_dtype),
        compiler_params=_cparams(1), name="moe_gather",
    )(idx, n_rows_valid, src)


def _expert_changes(eid_ref, i):
    return (i == 0) | (eid_ref[i] != eid_ref[jnp.maximum(i - 1, 0)])


def _moe_in_kernel(eid_ref, nvalid_ref, x_ref, wg_ref, wu_ref, o_ref, wb_ref, *, last_shift):
    j, i = pl.program_id(0), pl.program_id(1)
    tn = o_ref.shape[1]

    @pl.when(_expert_changes(eid_ref, i))
    def _():
        wb_ref[:, :tn] = wg_ref[...].astype(BF16)
        wb_ref[:, tn:] = wu_ref[...].astype(BF16)

    @pl.when(i < nvalid_ref[0])
    def _():
        gu = jnp.dot(x_ref[...], wb_ref[...], preferred_element_type=F32)
        act = (_silu(gu[:, :tn]) * gu[:, tn:]).astype(o_ref.dtype)
        if last_shift == 0:
            o_ref[...] = act
        else:
            is_last = j == pl.num_programs(0) - 1

            @pl.when(is_last)
            def _():
                o_ref[:, :tn - last_shift] = act[:, last_shift:]
                o_ref[:, tn - last_shift:] = jnp.zeros((act.shape[0], last_shift), o_ref.dtype)

            @pl.when(jnp.logical_not(is_last))
            def _():
                o_ref[...] = act

    @pl.when(i >= nvalid_ref[0])
    def _():
        o_ref[...] = jnp.zeros_like(o_ref)


def _moe_in(xs, w_in, tile_expert, n_valid, tm):
    r, d = xs.shape
    d_ff = w_in.shape[2] // 2
    tn = min(MOE_IN_TN, d_ff)
    nb = pl.cdiv(d_ff, tn)

    assert d_ff % LANES == 0 and tn % LANES == 0

    def col(j, base=0):
        return (jnp.minimum(j * (tn // LANES), (d_ff - tn) // LANES) + base // LANES) * LANES

    def x_map(j, i, eid, nv):
        return (jnp.minimum(i, nv[0] - 1), 0)

    return pl.pallas_call(
        functools.partial(_moe_in_kernel, last_shift=nb * tn - d_ff),
        grid_spec=pltpu.PrefetchScalarGridSpec(
            num_scalar_prefetch=2, grid=(nb, r // tm),
            in_specs=[pl.BlockSpec((tm, d), x_map),
                      pl.BlockSpec((None, pl.Element(d), pl.Element(tn)),
                                   lambda j, i, eid, nv: (eid[i], 0, col(j))),
                      pl.BlockSpec((None, pl.Element(d), pl.Element(tn)),
                                   lambda j, i, eid, nv: (eid[i], 0, col(j, d_ff)))],
            out_specs=pl.BlockSpec((tm, tn), lambda j, i, eid, nv: (i, j)),
            scratch_shapes=[pltpu.VMEM((d, 2 * tn), BF16)]),
        out_shape=jax.ShapeDtypeStruct((r, nb * tn), BF16),
        compiler_params=_cparams(2), name="moe_in",
    )(tile_expert, n_valid, xs, w_in, w_in)


def _moe_out_kernel(eid_ref, nvalid_ref, x_ref, w_ref, s_ref, o_ref, wb_ref):
    i = pl.program_id(1)

    @pl.when(_expert_changes(eid_ref, i))
    def _():
        wb_ref[...] = w_ref[...].astype(BF16)

    @pl.when(i < nvalid_ref[0])
    def _():
        acc = jnp.dot(x_ref[...], wb_ref[...], preferred_element_type=F32)
        o_ref[...] = acc * s_ref[...]

    @pl.when(i >= nvalid_ref[0])
    def _():
        o_ref[...] = jnp.zeros_like(o_ref)


def _moe_out(act, w_out, row_gate, tile_expert, n_valid, tm):
    r = act.shape[0]
    d_ff, d = w_out.shape[1:]
    tn = min(MOE_OUT_TN, d)

    def x_map(j, i, eid, nv):
        return (jnp.minimum(i, nv[0] - 1), 0)

    return pl.pallas_call(
        _moe_out_kernel,
        grid_spec=pltpu.PrefetchScalarGridSpec(
            num_scalar_prefetch=2, grid=(d // tn, r // tm),
            in_specs=[pl.BlockSpec((tm, d_ff), x_map),
                      pl.BlockSpec((None, d_ff, tn), lambda j, i, eid, nv: (eid[i], 0, j)),
                      pl.BlockSpec((tm, 1), x_map)],
            out_specs=pl.BlockSpec((tm, tn), lambda j, i, eid, nv: (i, j)),
            scratch_shapes=[pltpu.VMEM((d_ff, tn), BF16)]),
        out_shape=jax.ShapeDtypeStruct((r, d), F32),
        compiler_params=_cparams(2), name="moe_out",
    )(tile_expert, n_valid, act, w_out, row_gate.reshape(r, 1))


def _combine_kernel(dest_ref, x_ref, y_hbm, g_ref, o_ref, a_ref, b_ref, sem):
    rows = x_ref.shape[0]
    base = pl.program_id(0) * rows

    def start(r, _):
        _row_copy(y_hbm, a_ref, sem, dest_ref[2 * (base + r)], r).start()
        _row_copy(y_hbm, b_ref, sem, dest_ref[2 * (base + r) + 1], r).start()
        return 0

    def wait(r, _):
        _row_copy(y_hbm, a_ref, sem, 0, r).wait()
        _row_copy(y_hbm, b_ref, sem, 0, r).wait()
        return 0

    lax.fori_loop(0, rows, start, 0)
    lax.fori_loop(0, rows, wait, 0)
    x = x_ref[...] + (a_ref[...] + b_ref[...])
    ms = jnp.mean(x * x, axis=-1, keepdims=True)
    o_ref[...] = x * lax.rsqrt(ms + RMS_EPS) * g_ref[...]


def _combine_norm(x, y, dest, g):
    m, d = x.shape
    rows = min(COMBINE_ROWS, m)
    return pl.pallas_call(
        _combine_kernel,
        grid_spec=pltpu.PrefetchScalarGridSpec(
            num_scalar_prefetch=1, grid=(m // rows,),
            in_specs=[pl.BlockSpec((rows, d), lambda i, dest: (i, 0)),
                      pl.BlockSpec(memory_space=pl.ANY),
                      pl.BlockSpec((1, d), lambda i, dest: (0, 0))],
            out_specs=pl.BlockSpec((rows, d), lambda i, dest: (i, 0)),
            scratch_shapes=[pltpu.VMEM((rows, d), F32), pltpu.VMEM((rows, d), F32),
                            pltpu.SemaphoreType.DMA(())]),
        out_shape=jax.ShapeDtypeStruct((m, d), F32),
        compiler_params=_cparams(1), name="moe_combine_norm",
    )(dest, x, y, g.reshape(1, d))


def _dispatch_plan(top, n_experts, tm):
    m = top.shape[0]
    n_pairs = m * TOP_K
    expert = top[:, :TOP_K].astype(jnp.int32).reshape(n_pairs)
    gate = top[:, TOP_K:2 * TOP_K].reshape(n_pairs)
    onehot = (expert[:, None] == jnp.arange(n_experts, dtype=jnp.int32)[None, :]).astype(jnp.int32)
    before = jnp.cumsum(onehot, axis=0) - onehot
    rank = jnp.sum(before * onehot, axis=1)
    counts = jnp.sum(onehot, axis=0)
    tiles = (counts + tm - 1) // tm
    tile_end = jnp.cumsum(tiles)
    group_start = (tile_end - tiles) * tm
    dest = group_start[expert] + rank
    n_tiles = n_pairs // tm + n_experts
    n_rows = n_tiles * tm
    src_token = jnp.zeros((n_rows,), jnp.int32).at[dest].set(jnp.arange(n_pairs, dtype=jnp.int32) // TOP_K)
    row_gate = jnp.zeros((n_rows,), F32).at[dest].set(gate)
    tile_id = jnp.arange(n_tiles, dtype=jnp.int32)
    tile_expert = jnp.minimum(jnp.sum((tile_end[None, :] <= tile_id[:, None]).astype(jnp.int32), axis=1),
                              n_experts - 1)
    n_valid = tile_end[-1:].astype(jnp.int32)
    return dest.astype(jnp.int32), src_token, row_gate, tile_expert, n_valid


def kernel(x, attn_norm, ffn_norm, gla_w_in, gla_w_gate, gla_b_gate, gla_onorm, gla_w_out,
           sb_w_in, sb_w_out, dense_w_in, dense_w_out, moe_router, moe_w_in, moe_w_out,
           final_norm):
    batch, seq, d = x.shape
    m = batch * seq
    x = x.reshape(m, d)

    rank, dk = gla_w_gate.shape[1:]
    hv = gla_onorm.shape[1]
    dv = GLA_HEADS * hv
    n_proj = 2 * dk + 2 * dv
    h = _rmsnorm(x, attn_norm[0], BF16)
    w_in_t = jnp.transpose(gla_w_in[0])
    proj = _matmul_nt(h, w_in_t, n_cols=n_proj, name="gla_in")
    b = _gla_gate(h, w_in_t, n_proj, gla_w_gate[0], gla_b_gate[0], min(GLA_CHUNK, seq))
    o = _gla_core(proj, b, gla_onorm[0], batch, seq, GLA_HEADS, dk, dv)
    x = _matmul(o, gla_w_out[0], n_cols=d, tk=dv, res=x, name="gla_out")

    d_ff = dense_w_out.shape[1]
    h = _rmsnorm(x, ffn_norm[0], BF16)
    act = _swiglu_in(h, dense_w_in[0], d_ff, name="dense_in")
    half = d_ff // 2
    x = _matmul(act, dense_w_out[0], n_cols=d, tk=half, k_blk=0, tn=256, res=x, name="dense_out0")
    x = _matmul(act, dense_w_out[0], n_cols=d, tk=half, k_blk=1, tn=256, res=x, name="dense_out1")

    h = _rmsnorm(x, attn_norm[1], BF16)
    qkv = _matmul(h, sb_w_in[0], n_cols=3 * d, tk=d, name="sb_in")
    o = _sb_core(qkv, batch, seq, SB_HEADS, d)
    x = _matmul(o, sb_w_out[0], n_cols=d, tk=d, res=x, name="sb_out")

    n_experts = moe_router.shape[2]
    tm = min(MOE_TM, m)
    h32, top = _router(x, ffn_norm[1], moe_router[0])
    dest, src_token, row_gate, tile_expert, n_valid = _dispatch_plan(top, n_experts, tm)
    xs = _gather_rows(h32, src_token, n_valid * tm, BF16)
    act = _moe_in(xs, moe_w_in[0], tile_expert, n_valid, tm)
    y = _moe_out(act, moe_w_out[0], row_gate, tile_expert, n_valid, tm)
    out = _combine_norm(x, y, dest, final_norm)
    return out.reshape(batch, seq, d)
```

```python
import functools

import jax
import jax.numpy as jnp
from jax import lax
from jax.experimental import pallas as pl
from jax.experimental.pallas import tpu as pltpu

F32 = jnp.float32
BF16 = jnp.bfloat16

RMS_EPS = 1e-6
LOG2_E = 1.4426950408889634
GLA_HEADS = 4
GLA_GATE_TAU = 16.0
GLA_CHUNK = 64
GLA_SUB = 16
SB_HEADS = 32
SB_TQ = 512
SB_TK = 256
TOP_K = 2
LANES = 128
VMEM_LIMIT_BYTES = 56 * 1024 * 1024

MM_TM = 1024
MM_TN = 512
MOE_TM = 512
MOE_IN_TN = 512
MOE_OUT_TN = 512
GATHER_ROWS = 512
COMBINE_ROWS = 256


def _cparams(n_axes):
    return pltpu.CompilerParams(
        dimension_semantics=("arbitrary",) * n_axes,
        vmem_limit_bytes=VMEM_LIMIT_BYTES)


def _silu(x):
    return x / (1.0 + jnp.exp(-x))


def _log_sigmoid(x):
    return jnp.minimum(x, 0.0) - jnp.log1p(jnp.exp(-jnp.abs(x)))


def _rmsnorm_kernel(x_ref, g_ref, o_ref):
    x = x_ref[...]
    ms = jnp.mean(x * x, axis=-1, keepdims=True)
    o_ref[...] = (x * lax.rsqrt(ms + RMS_EPS) * g_ref[...]).astype(o_ref.dtype)


def _rmsnorm(x, g, out_dtype):
    m, d = x.shape
    tm = min(256, m)
    return pl.pallas_call(
        _rmsnorm_kernel,
        grid=(m // tm,),
        in_specs=[pl.BlockSpec((tm, d), lambda i: (i, 0)),
                  pl.BlockSpec((1, d), lambda i: (0, 0))],
        out_specs=pl.BlockSpec((tm, d), lambda i: (i, 0)),
        out_shape=jax.ShapeDtypeStruct((m, d), out_dtype),
        compiler_params=_cparams(1),
        name="rmsnorm",
    )(x, g.reshape(1, d))


def _mm_plain_kernel(x_ref, w_ref, o_ref):
    acc = jnp.dot(x_ref[...], w_ref[...].astype(BF16), preferred_element_type=F32)
    o_ref[...] = acc.astype(o_ref.dtype)


def _mm_nt_kernel(x_ref, wt_ref, o_ref):
    acc = lax.dot_general(x_ref[...], wt_ref[...].astype(BF16), (((1,), (1,)), ((), ())),
                          preferred_element_type=F32)
    o_ref[...] = acc.astype(o_ref.dtype)


def _matmul_nt(x, wt, *, n_cols, tm=None, tn=None, out_dtype=BF16, name="matmul_nt"):
    m, k = x.shape
    tm = min(tm or MM_TM, m)
    tn = min(tn or MM_TN, n_cols)
    assert m % tm == 0 and n_cols % tn == 0
    return pl.pallas_call(
        _mm_nt_kernel, grid=(m // tm, n_cols // tn),
        in_specs=[pl.BlockSpec((tm, k), lambda i, j: (i, 0)),
                  pl.BlockSpec((tn, k), lambda i, j: (j, 0))],
        out_specs=pl.BlockSpec((tm, tn), lambda i, j: (i, j)),
        out_shape=jax.ShapeDtypeStruct((m, n_cols), out_dtype),
        compiler_params=_cparams(2), name=name,
    )(x, wt)


def _mm_res_kernel(x_ref, w_ref, res_ref, o_ref):
    acc = jnp.dot(x_ref[...], w_ref[...].astype(BF16), preferred_element_type=F32)
    o_ref[...] = res_ref[...] + acc


def _mm_swiglu_kernel(x_ref, wg_ref, wu_ref, o_ref):
    x = x_ref[...]
    g = jnp.dot(x, wg_ref[...].astype(BF16), preferred_element_type=F32)
    u = jnp.dot(x, wu_ref[...].astype(BF16), preferred_element_type=F32)
    o_ref[...] = (_silu(g) * u).astype(o_ref.dtype)


def _matmul(x, w, *, n_cols, tk, k_blk=0, w_col_blk=0, tm=None, tn=None,
            out_dtype=BF16, res=None, name="matmul"):
    m = x.shape[0]
    tm = min(tm or MM_TM, m)
    tn = min(tn or MM_TN, n_cols)
    assert m % tm == 0 and n_cols % tn == 0
    grid = (m // tm, n_cols // tn)
    x_spec = pl.BlockSpec((tm, tk), lambda i, j: (i, k_blk))
    w_spec = pl.BlockSpec((tk, tn), lambda i, j: (k_blk, w_col_blk + j))
    o_spec = pl.BlockSpec((tm, tn), lambda i, j: (i, j))
    if res is None:
        kern, in_specs, args = _mm_plain_kernel, [x_spec, w_spec], (x, w)
    else:
        kern, in_specs, args = _mm_res_kernel, [x_spec, w_spec, o_spec], (x, w, res)
        out_dtype = F32
    return pl.pallas_call(
        kern, grid=grid, in_specs=in_specs, out_specs=o_spec,
        out_shape=jax.ShapeDtypeStruct((m, n_cols), out_dtype),
        compiler_params=_cparams(2), name=name,
    )(*args)


def _swiglu_in(x, w, d_ff, *, tm=None, tn=None, name="swiglu_in"):
    m, k = x.shape
    tm = min(tm or MM_TM, m)
    tn = tn or 256
    assert m % tm == 0 and d_ff % tn == 0
    nb = d_ff // tn
    return pl.pallas_call(
        _mm_swiglu_kernel,
        grid=(m // tm, nb),
        in_specs=[pl.BlockSpec((tm, k), lambda i, j: (i, 0)),
                  pl.BlockSpec((k, tn), lambda i, j: (0, j)),
                  pl.BlockSpec((k, tn), lambda i, j: (0, nb + j))],
        out_specs=pl.BlockSpec((tm, tn), lambda i, j: (i, j)),
        out_shape=jax.ShapeDtypeStruct((m, d_ff), BF16),
        compiler_params=_cparams(2), name=name,
    )(x, w, w)


def _dot_f32(a, b):
    return jnp.dot(a, b, preferred_element_type=F32, precision=lax.Precision.HIGHEST)


def _gla_gate_kernel(h_ref, wa_ref, wg_ref, bg_ref, o_ref, *, chunk, rank):
    tm, d = h_ref.shape
    w_a = jnp.concatenate([wa_ref[...].astype(BF16), jnp.zeros((LANES - rank, d), BF16)], axis=0)
    a_low = lax.dot_general(h_ref[...], w_a, (((1,), (1,)), ((), ())),
                            preferred_element_type=F32)
    xg = _dot_f32(a_low, wg_ref[...]) + bg_ref[...]
    log_alpha = _log_sigmoid(xg) * (1.0 / GLA_GATE_TAU)
    row = lax.broadcasted_iota(jnp.int32, (tm, tm), 0)
    col = lax.broadcasted_iota(jnp.int32, (tm, tm), 1)
    same_chunk = (row // chunk) == (col // chunk)
    tril = jnp.where(same_chunk & (col <= row), 1.0, 0.0).astype(F32)
    o_ref[...] = _dot_f32(tril, log_alpha)


def _gla_gate(h, w_in_t, n_proj, w_gate, b_gate, chunk):
    m, d = h.shape
    rank, dk = w_gate.shape
    assert n_proj % rank == 0 and rank % 16 == 0 and rank <= LANES
    tm = min(256, m)
    wg_pad = jnp.zeros((LANES, dk), F32).at[:rank, :].set(w_gate)
    return pl.pallas_call(
        functools.partial(_gla_gate_kernel, chunk=chunk, rank=rank),
        grid=(m // tm,),
        in_specs=[pl.BlockSpec((tm, d), lambda i: (i, 0)),
                  pl.BlockSpec((rank, d), lambda i: (n_proj // rank, 0)),
                  pl.BlockSpec((LANES, dk), lambda i: (0, 0)),
                  pl.BlockSpec((1, dk), lambda i: (0, 0))],
        out_specs=pl.BlockSpec((tm, dk), lambda i: (i, 0)),
        out_shape=jax.ShapeDtypeStruct((m, dk), F32),
        compiler_params=_cparams(1), name="gla_gate",
    )(h, w_in_t, wg_pad, b_gate.reshape(1, dk))


def _gla_core_kernel(q_ref, k_ref, v_ref, r_ref, b_ref, gain_ref, o_ref, s_ref, *, scale, sub):
    chunk, hk = q_ref.shape

    @pl.when(pl.program_id(2) == 0)
    def _():
        s_ref[...] = jnp.zeros_like(s_ref)

    b = b_ref[...]
    q = q_ref[...].astype(F32) * scale
    k = k_ref[...].astype(F32)
    v = v_ref[...]
    state = s_ref[...]

    o = jnp.dot((q * jnp.exp(b)).astype(BF16), state.astype(BF16), preferred_element_type=F32)

    row_id = lax.broadcasted_iota(jnp.int32, (sub, chunk), 0)
    key_id = lax.broadcasted_iota(jnp.int32, (sub, chunk), 1)
    score_rows = []
    for i in range(chunk // sub):
        lo = i * sub
        b_i = b[lo:lo + sub, :]
        q_i = q[lo:lo + sub, :]
        diag = jnp.zeros((sub, chunk), F32)
        for j in range(sub):
            s = lo + j
            decay = jnp.exp(jnp.minimum(b_i - b[s:s + 1, :], 0.0))
            col = jnp.sum(q_i * decay * k[s:s + 1, :], axis=-1, keepdims=True)
            diag = jnp.where(key_id == s, col, diag)
        scores = jnp.where(key_id - lo <= row_id, diag, 0.0)
        if i > 0:
            b_first = b[lo:lo + 1, :]
            q_t = (q_i * jnp.exp(b_i - b_first)).astype(BF16)
            k_t = (k * jnp.exp(jnp.minimum(b_first - b, 0.0))).astype(BF16)
            below = lax.dot_general(q_t, k_t, (((1,), (1,)), ((), ())), preferred_element_type=F32)
            scores = jnp.where(key_id < lo, below, scores)
        score_rows.append(scores)
    scores = jnp.concatenate(score_rows, axis=0).astype(BF16)
    o = o + jnp.dot(scores, v, preferred_element_type=F32)

    b_last = b[chunk - 1:chunk, :]
    k_state = (k * jnp.exp(b_last - b)).astype(BF16)
    update = lax.dot_general(k_state, v, (((0,), (0,)), ((), ())), preferred_element_type=F32)
    decay_rows = jnp.broadcast_to(jnp.exp(b_last), (LANES, hk))
    decay_col = jnp.transpose(decay_rows)[:, 0:1]
    s_ref[...] = state * decay_col + update

    ms = jnp.mean(o * o, axis=-1, keepdims=True)
    o = o * lax.rsqrt(ms + RMS_EPS) * gain_ref[...]
    o_ref[...] = (o * _silu(r_ref[...].astype(F32))).astype(o_ref.dtype)


def _gla_core(proj, b, gain, batch, seq, heads, dk, dv):
    m = batch * seq
    hk, hv = dk // heads, dv // heads
    chunk = min(GLA_CHUNK, seq)
    nc = seq // chunk
    k_blk0 = dk // hk
    v_blk0 = (2 * dk) // hv
    r_blk0 = (2 * dk + dv) // hv

    def rows(bi, ci):
        return bi * nc + ci

    return pl.pallas_call(
        functools.partial(_gla_core_kernel, scale=float(hk) ** -0.5, sub=min(GLA_SUB, chunk)),
        grid=(batch, heads, nc),
        in_specs=[pl.BlockSpec((chunk, hk), lambda bi, h, c: (rows(bi, c), h)),
                  pl.BlockSpec((chunk, hk), lambda bi, h, c: (rows(bi, c), k_blk0 + h)),
                  pl.BlockSpec((chunk, hv), lambda bi, h, c: (rows(bi, c), v_blk0 + h)),
                  pl.BlockSpec((chunk, hv), lambda bi, h, c: (rows(bi, c), r_blk0 + h)),
                  pl.BlockSpec((chunk, hk), lambda bi, h, c: (rows(bi, c), h)),
                  pl.BlockSpec((1, hv), lambda bi, h, c: (0, 0))],
        out_specs=pl.BlockSpec((chunk, hv), lambda bi, h, c: (rows(bi, c), h)),
        out_shape=jax.ShapeDtypeStruct((m, dv), BF16),
        scratch_shapes=[pltpu.VMEM((hk, hv), F32)],
        compiler_params=_cparams(3), name="gla_core",
    )(proj, proj, proj, proj, b, gain.reshape(1, hv))


def _sb_kernel(q_ref, k_ref, v_ref, o_ref, acc_ref, carry_ref, hl0_ref, hl1_ref, lsp0_ref, lsp1_ref,
               sl0_ref, sl1_ref, tot0_ref, tot1_ref, *, scale, tk):
    tq, dh = q_ref.shape
    ratio = tq // tk
    assert ratio % 2 == 0
    hl_refs, lsp_refs = (hl0_ref, hl1_ref), (lsp0_ref, lsp1_ref)
    sl_refs, tot_refs = (sl0_ref, sl1_ref), (tot0_ref, tot1_ref)
    qi = pl.program_id(2)
    q = (q_ref[...].astype(F32) * (scale * LOG2_E)).astype(BF16)

    later = lax.broadcasted_iota(jnp.int32, (tk, tk), 0)
    key = lax.broadcasted_iota(jnp.int32, (tk, tk), 1)
    suffix_ones = jnp.where((later > key) | (key == tk - 1), 1.0, 0.0).astype(BF16)

    acc_ref[...] = jnp.zeros_like(acc_ref)
    carry_ref[...] = jnp.zeros_like(carry_ref)

    def logit_stage(k_start, parity, strict):
        k_blk = k_ref[pl.ds(k_start, tk), :]
        z = lax.dot_general(q, k_blk, (((1,), (1,)), ((), ())), preferred_element_type=F32)
        neg_abs = pltpu.bitcast(pltpu.bitcast(z, jnp.uint32) | jnp.uint32(0x80000000), F32)
        ls_pos = jnp.minimum(z, 0.0) - jnp.log(1.0 + jnp.exp2(neg_abs)) * LOG2_E
        log_1m = ls_pos - z
        if strict is not None:
            log_1m = jnp.where(strict, log_1m, 0.0)
        hl_refs[parity][...] = log_1m.astype(BF16)
        lsp_refs[parity][...] = ls_pos

    def suffix_stage(parity):
        sums = jnp.dot(hl_refs[parity][...], suffix_ones, preferred_element_type=F32)
        sl_refs[parity][...] = lsp_refs[parity][...] + jnp.where(lane == tk - 1, 0.0, sums)
        tot_refs[parity][...] = jnp.broadcast_to(sums[:, tk - 1:tk], (tq, LANES))

    def value_stage(k_start, parity, strict):
        v_blk = v_ref[pl.ds(k_start, tk), :]
        carry = carry_ref[...]
        w = jnp.exp2(sl_refs[parity][...] + jnp.concatenate([carry] * (tk // LANES), axis=1))
        if strict is not None:
            w = jnp.where(strict, w, 0.0)
        acc_ref[...] += jnp.dot(w.astype(BF16), v_blk, preferred_element_type=F32)
        carry_ref[...] = carry + tot_refs[parity][...]

    def k_start_of(t):
        return pl.multiple_of(jnp.maximum(qi * tq + (ratio - 1 - t) * tk, 0), tk)

    row = lax.broadcasted_iota(jnp.int32, (tq, tk), 0)
    lane = lax.broadcasted_iota(jnp.int32, (tq, tk), 1)

    def strict_mask(t):
        return (lane + (ratio - 1 - t) * tk) < row if t < ratio else None

    def step(s, parity, mask_s, mask_ahead):
        logit_stage(k_start_of(s + 2), parity, mask_ahead)
        suffix_stage(1 - parity)
        value_stage(k_start_of(s), parity, mask_s)

    logit_stage(k_start_of(0), 0, strict_mask(0))
    logit_stage(k_start_of(1), 1, strict_mask(1))
    suffix_stage(0)
    for s in range(ratio):
        step(s, s % 2, strict_mask(s), strict_mask(s + 2))

    def body(it, _):
        for j in range(ratio):
            step(ratio + it * ratio + j, j % 2, None, None)
        return 0

    lax.fori_loop(0, qi, body, 0)
    o_ref[...] = acc_ref[...].astype(o_ref.dtype)


def _sb_core(qkv, batch, seq, heads, d_model):
    dh = d_model // heads
    tq = min(SB_TQ, seq)
    tk = min(SB_TK, tq)
    nq = seq // tq
    return pl.pallas_call(
        functools.partial(_sb_kernel, scale=float(dh) ** -0.5, tk=tk),
        grid=(batch, heads, nq),
        in_specs=[pl.BlockSpec((tq, dh), lambda b, h, i: (b * nq + i, h)),
                  pl.BlockSpec((seq, dh), lambda b, h, i: (b, heads + h)),
                  pl.BlockSpec((seq, dh), lambda b, h, i: (b, 2 * heads + h))],
        out_specs=pl.BlockSpec((tq, dh), lambda b, h, i: (b * nq + i, h)),
        out_shape=jax.ShapeDtypeStruct((batch * seq, d_model), BF16),
        scratch_shapes=[pltpu.VMEM((tq, dh), F32), pltpu.VMEM((tq, LANES), F32),
                        pltpu.VMEM((tq, tk), BF16), pltpu.VMEM((tq, tk), BF16),
                        pltpu.VMEM((tq, tk), F32), pltpu.VMEM((tq, tk), F32),
                        pltpu.VMEM((tq, tk), F32), pltpu.VMEM((tq, tk), F32),
                        pltpu.VMEM((tq, LANES), F32), pltpu.VMEM((tq, LANES), F32)],
        compiler_params=_cparams(3), name="sb_core",
    )(qkv, qkv, qkv)


def _router_kernel(x_ref, g_ref, wr_ref, h_ref, top_ref, *, n_experts):
    x = x_ref[...]
    ms = jnp.mean(x * x, axis=-1, keepdims=True)
    h = x * lax.rsqrt(ms + RMS_EPS) * g_ref[...]
    h_ref[...] = h
    logits = _dot_f32(h, wr_ref[...])
    lane = lax.broadcasted_iota(jnp.int32, logits.shape, 1)
    logits = jnp.where(lane < n_experts, logits, -jnp.inf)
    v1 = jnp.max(logits, axis=-1, keepdims=True)
    i1 = jnp.min(jnp.where(logits == v1, lane, LANES), axis=-1, keepdims=True)
    rest = jnp.where(lane == i1, -jnp.inf, logits)
    v2 = jnp.max(rest, axis=-1, keepdims=True)
    i2 = jnp.min(jnp.where(rest == v2, lane, LANES), axis=-1, keepdims=True)
    e = jnp.exp(v2 - v1)
    g1 = 1.0 / (1.0 + e)
    g2 = e * g1
    out = jnp.where(lane == 0, i1.astype(F32),
                    jnp.where(lane == 1, i2.astype(F32),
                              jnp.where(lane == 2, g1, jnp.where(lane == 3, g2, 0.0))))
    top_ref[...] = out


def _router(x, g, w_router):
    m, d = x.shape
    n_experts = w_router.shape[1]
    tm = min(256, m)
    wr_pad = jnp.zeros((d, LANES), F32).at[:, :n_experts].set(w_router)
    return pl.pallas_call(
        functools.partial(_router_kernel, n_experts=n_experts),
        grid=(m // tm,),
        in_specs=[pl.BlockSpec((tm, d), lambda i: (i, 0)),
                  pl.BlockSpec((1, d), lambda i: (0, 0)),
                  pl.BlockSpec((d, LANES), lambda i: (0, 0))],
        out_specs=[pl.BlockSpec((tm, d), lambda i: (i, 0)),
                   pl.BlockSpec((tm, LANES), lambda i: (i, 0))],
        out_shape=[jax.ShapeDtypeStruct((m, d), F32),
                   jax.ShapeDtypeStruct((m, LANES), F32)],
        compiler_params=_cparams(1), name="moe_router",
    )(x, g.reshape(1, d), wr_pad)


def _row_copy(src_hbm, dst_ref, sem, src_row, dst_row):
    return pltpu.make_async_copy(src_hbm.at[pl.ds(src_row, 1), :],
                                 dst_ref.at[pl.ds(dst_row, 1), :], sem)


def _gather_kernel(idx_ref, nrows_ref, src_hbm, o_ref, buf_ref, sem):
    rows = o_ref.shape[0]
    base = pl.program_id(0) * rows

    @pl.when(base < nrows_ref[0])
    def _():
        def start(r, _):
            _row_copy(src_hbm, buf_ref, sem, idx_ref[base + r], r).start()
            return 0

        def wait(r, _):
            _row_copy(src_hbm, buf_ref, sem, 0, r).wait()
            return 0

        lax.fori_loop(0, rows, start, 0)
        lax.fori_loop(0, rows, wait, 0)
        o_ref[...] = buf_ref[...].astype(o_ref.dtype)

    @pl.when(base >= nrows_ref[0])
    def _():
        o_ref[...] = jnp.zeros_like(o_ref)


def _gather_rows(src, idx, n_rows_valid, out_dtype):
    n = idx.shape[0]
    d = src.shape[1]
    rows = min(GATHER_ROWS, n)
    assert n % rows == 0
    return pl.pallas_call(
        _gather_kernel,
        grid_spec=pltpu.PrefetchScalarGridSpec(
            num_scalar_prefetch=2, grid=(n // rows,),
            in_specs=[pl.BlockSpec(memory_space=pl.ANY)],
            out_specs=pl.BlockSpec((rows, d), lambda i, idx, nr: (i, 0)),
            scratch_shapes=[pltpu.VMEM((rows, d), src.dtype), pltpu.SemaphoreType.DMA(())]),
        out_shape=jax.ShapeDtypeStruct((n, d), out_dtype),
        compiler_params=_cparams(1), name="moe_gather",
    )(idx, n_rows_valid, src)


def _weight_stream(plan_ref, copies, refill):
    j, i = pl.program_id(0), pl.program_id(1)

    @pl.when((j == 0) & (i == 0))
    def _():
        for cp in copies(plan_ref[0, 0], 0):
            cp.start()

    @pl.when(plan_ref[1, i] == 1)
    def _():
        for cp in copies(plan_ref[0, i], j):
            cp.wait()
        refill()
        j_next = j + plan_ref[3, i]

        @pl.when(j_next < pl.num_programs(0))
        def _():
            for cp in copies(plan_ref[2, i], j_next):
                cp.start()


def _moe_in_kernel(plan_ref, nvalid_ref, x_ref, w_hbm, o_ref, wf_ref, wb_ref, sem, *, d_ff, last_shift):
    j, i = pl.program_id(0), pl.program_id(1)
    tn = o_ref.shape[1]

    def copies(expert, jb):
        c = pl.multiple_of(jnp.minimum(jb * (tn // LANES), (d_ff - tn) // LANES) * LANES, LANES)
        return (pltpu.make_async_copy(w_hbm.at[expert, :, pl.ds(c, tn)], wf_ref.at[:, pl.ds(0, tn)], sem.at[0]),
                pltpu.make_async_copy(w_hbm.at[expert, :, pl.ds(pl.multiple_of(d_ff + c, LANES), tn)],
                                      wf_ref.at[:, pl.ds(tn, tn)], sem.at[1]))

    def refill():
        wb_ref[...] = wf_ref[...].astype(BF16)

    _weight_stream(plan_ref, copies, refill)

    @pl.when(i < nvalid_ref[0])
    def _():
        gu = jnp.dot(x_ref[...], wb_ref[...], preferred_element_type=F32)
        act = (_silu(gu[:, :tn]) * gu[:, tn:]).astype(o_ref.dtype)
        if last_shift == 0:
            o_ref[...] = act
        else:
            is_last = j == pl.num_programs(0) - 1

            @pl.when(is_last)
            def _():
                o_ref[:, :tn - last_shift] = act[:, last_shift:]
                o_ref[:, tn - last_shift:] = jnp.zeros((act.shape[0], last_shift), o_ref.dtype)

            @pl.when(jnp.logical_not(is_last))
            def _():
                o_ref[...] = act

    @pl.when(i >= nvalid_ref[0])
    def _():
        o_ref[...] = jnp.zeros_like(o_ref)


def _moe_in(xs, w_in, plan, n_valid, tm):
    r, d = xs.shape
    d_ff = w_in.shape[2] // 2
    tn = min(MOE_IN_TN, d_ff)
    nb = pl.cdiv(d_ff, tn)
    assert d_ff % LANES == 0 and tn % LANES == 0

    def x_map(j, i, plan, nv):
        return (jnp.minimum(i, nv[0] - 1), 0)

    return pl.pallas_call(
        functools.partial(_moe_in_kernel, d_ff=d_ff, last_shift=nb * tn - d_ff),
        grid_spec=pltpu.PrefetchScalarGridSpec(
            num_scalar_prefetch=2, grid=(nb, r // tm),
            in_specs=[pl.BlockSpec((tm, d), x_map),
                      pl.BlockSpec(memory_space=pl.ANY)],
            out_specs=pl.BlockSpec((tm, tn), lambda j, i, plan, nv: (i, j)),
            scratch_shapes=[pltpu.VMEM((d, 2 * tn), F32), pltpu.VMEM((d, 2 * tn), BF16),
                            pltpu.SemaphoreType.DMA((2,))]),
        out_shape=jax.ShapeDtypeStruct((r, nb * tn), BF16),
        compiler_params=_cparams(2), name="moe_in",
    )(plan, n_valid, xs, w_in)


def _moe_out_kernel(plan_ref, nvalid_ref, x_ref, w_hbm, o_ref, wf_ref, wb_ref, sem):
    i = pl.program_id(1)
    tn = o_ref.shape[1]

    def copies(expert, jb):
        c = pl.multiple_of(jb * tn, LANES)
        return (pltpu.make_async_copy(w_hbm.at[expert, :, pl.ds(c, tn)], wf_ref, sem.at[0]),)

    def refill():
        wb_ref[...] = wf_ref[...].astype(BF16)

    _weight_stream(plan_ref, copies, refill)

    @pl.when(i < nvalid_ref[0])
    def _():
        o_ref[...] = jnp.dot(x_ref[...], wb_ref[...], preferred_element_type=F32)

    @pl.when(i >= nvalid_ref[0])
    def _():
        o_ref[...] = jnp.zeros_like(o_ref)


def _moe_out(act, w_out, plan, n_valid, tm):
    r = act.shape[0]
    d_ff, d = w_out.shape[1:]
    tn = min(MOE_OUT_TN, d)
    assert d % tn == 0 and tn % LANES == 0

    def x_map(j, i, plan, nv):
        return (jnp.minimum(i, nv[0] - 1), 0)

    return pl.pallas_call(
        _moe_out_kernel,
        grid_spec=pltpu.PrefetchScalarGridSpec(
            num_scalar_prefetch=2, grid=(d // tn, r // tm),
            in_specs=[pl.BlockSpec((tm, d_ff), x_map),
                      pl.BlockSpec(memory_space=pl.ANY)],
            out_specs=pl.BlockSpec((tm, tn), lambda j, i, plan, nv: (i, j)),
            scratch_shapes=[pltpu.VMEM((d_ff, tn), F32), pltpu.VMEM((d_ff, tn), BF16),
                            pltpu.SemaphoreType.DMA((1,))]),
        out_shape=jax.ShapeDtypeStruct((r, d), F32),
        compiler_params=_cparams(2), name="moe_out",
    )(plan, n_valid, act, w_out)


def _combine_kernel(dest_ref, x_ref, y_hbm, top_ref, g_ref, o_ref, a_ref, b_ref, sem):
    rows = x_ref.shape[0]
    base = pl.program_id(0) * rows

    def start(r, _):
        _row_copy(y_hbm, a_ref, sem, dest_ref[2 * (base + r)], r).start()
        _row_copy(y_hbm, b_ref, sem, dest_ref[2 * (base + r) + 1], r).start()
        return 0

    def wait(r, _):
        _row_copy(y_hbm, a_ref, sem, 0, r).wait()
        _row_copy(y_hbm, b_ref, sem, 0, r).wait()
        return 0

    lax.fori_loop(0, rows, start, 0)
    lax.fori_loop(0, rows, wait, 0)
    top = top_ref[...]
    moe = a_ref[...] * top[:, TOP_K:TOP_K + 1] + b_ref[...] * top[:, TOP_K + 1:TOP_K + 2]
    x = x_ref[...] + moe
    ms = jnp.mean(x * x, axis=-1, keepdims=True)
    o_ref[...] = x * lax.rsqrt(ms + RMS_EPS) * g_ref[...]


def _combine_norm(x, y, dest, top, g):
    m, d = x.shape
    rows = min(COMBINE_ROWS, m)
    return pl.pallas_call(
        _combine_kernel,
        grid_spec=pltpu.PrefetchScalarGridSpec(
            num_scalar_prefetch=1, grid=(m // rows,),
            in_specs=[pl.BlockSpec((rows, d), lambda i, dest: (i, 0)),
                      pl.BlockSpec(memory_space=pl.ANY),
                      pl.BlockSpec((rows, LANES), lambda i, dest: (i, 0)),
                      pl.BlockSpec((1, d), lambda i, dest: (0, 0))],
            out_specs=pl.BlockSpec((rows, d), lambda i, dest: (i, 0)),
            scratch_shapes=[pltpu.VMEM((rows, d), F32), pltpu.VMEM((rows, d), F32),
                            pltpu.SemaphoreType.DMA(())]),
        out_shape=jax.ShapeDtypeStruct((m, d), F32),
        compiler_params=_cparams(1), name="moe_combine_norm",
    )(dest, x, y, top, g.reshape(1, d))


def _dispatch_plan(top, n_experts, tm):
    m = top.shape[0]
    n_pairs = m * TOP_K
    expert = top[:, :TOP_K].astype(jnp.int32).reshape(n_pairs)
    onehot = (expert[:, None] == jnp.arange(n_experts, dtype=jnp.int32)[None, :]).astype(jnp.int32)
    before = jnp.cumsum(onehot, axis=0) - onehot
    rank = jnp.sum(before * onehot, axis=1)
    counts = jnp.sum(onehot, axis=0)
    tiles = (counts + tm - 1) // tm
    tile_end = jnp.cumsum(tiles)
    group_start = (tile_end - tiles) * tm
    dest = group_start[expert] + rank
    n_tiles = n_pairs // tm + n_experts
    n_rows = n_tiles * tm
    src_token = jnp.zeros((n_rows,), jnp.int32).at[dest].set(jnp.arange(n_pairs, dtype=jnp.int32) // TOP_K)
    tile_id = jnp.arange(n_tiles, dtype=jnp.int32)
    tile_expert = jnp.minimum(jnp.sum((tile_end[None, :] <= tile_id[:, None]).astype(jnp.int32), axis=1),
                              n_experts - 1)
    n_valid = tile_end[-1:].astype(jnp.int32)
    prev_expert = jnp.concatenate([jnp.full((1,), -1, jnp.int32), tile_expert[:-1]])
    first = (tile_id < n_valid[0]) & (tile_expert != prev_expert)
    later_first = first[None, :] & (tile_id[None, :] > tile_id[:, None])
    next_first = jnp.min(jnp.where(later_first, tile_id[None, :], n_tiles), axis=1)
    is_last_run = next_first == n_tiles
    next_expert = jnp.where(is_last_run, tile_expert[0], tile_expert[jnp.minimum(next_first, n_tiles - 1)])
    plan = jnp.stack([tile_expert, first.astype(jnp.int32), next_expert, is_last_run.astype(jnp.int32)])
    return dest.astype(jnp.int32), src_token, plan.astype(jnp.int32), n_valid


def kernel(x, attn_norm, ffn_norm, gla_w_in, gla_w_gate, gla_b_gate, gla_onorm, gla_w_out,
           sb_w_in, sb_w_out, dense_w_in, dense_w_out, moe_router, moe_w_in, moe_w_out,
           final_norm):
    batch, seq, d = x.shape
    m = batch * seq
    x = x.reshape(m, d)

    rank, dk = gla_w_gate.shape[1:]
    hv = gla_onorm.shape[1]
    dv = GLA_HEADS * hv
    n_proj = 2 * dk + 2 * dv
    h = _rmsnorm(x, attn_norm[0], BF16)
    w_in_t = jnp.transpose(gla_w_in[0])
    proj = _matmul_nt(h, w_in_t, n_cols=n_proj, name="gla_in")
    b = _gla_gate(h, w_in_t, n_proj, gla_w_gate[0], gla_b_gate[0], min(GLA_CHUNK, seq))
    o = _gla_core(proj, b, gla_onorm[0], batch, seq, GLA_HEADS, dk, dv)
    x = _matmul(o, gla_w_out[0], n_cols=d, tk=dv, res=x, name="gla_out")

    d_ff = dense_w_out.shape[1]
    h = _rmsnorm(x, ffn_norm[0], BF16)
    act = _swiglu_in(h, dense_w_in[0], d_ff, name="dense_in")
    half = d_ff // 2
    x = _matmul(act, dense_w_out[0], n_cols=d, tk=half, k_blk=0, tn=256, res=x, name="dense_out0")
    x = _matmul(act, dense_w_out[0], n_cols=d, tk=half, k_blk=1, tn=256, res=x, name="dense_out1")

    h = _rmsnorm(x, attn_norm[1], BF16)
    qkv = _matmul(h, sb_w_in[0], n_cols=3 * d, tk=d, name="sb_in")
    o = _sb_core(qkv, batch, seq, SB_HEADS, d)
    x = _matmul(o, sb_w_out[0], n_cols=d, tk=d, res=x, name="sb_out")

    n_experts = moe_router.shape[2]
    tm = min(MOE_TM, m)
    h32, top = _router(x, ffn_norm[1], moe_router[0])
    dest, src_token, plan, n_valid = _dispatch_plan(top, n_experts, tm)
    xs = _gather_rows(h32, src_token, n_valid * tm, BF16)
    act = _moe_in(xs, moe_w_in[0], plan, n_valid, tm)
    y = _moe_out(act, moe_w_out[0], plan, n_valid, tm)
    out = _combine_norm(x, y, dest, top, final_norm)
    return out.reshape(batch, seq, d)
```

```python
import functools

import jax
import jax.numpy as jnp
from jax import lax
from jax.experimental import pallas as pl
from jax.experimental.pallas import tpu as pltpu

F32 = jnp.float32
BF16 = jnp.bfloat16

RMS_EPS = 1e-6
LOG2_E = 1.4426950408889634
GLA_HEADS = 4
GLA_GATE_TAU = 16.0
GLA_CHUNK = 64
GLA_SUB = 16
GLA_MILD_DECAY = 60.0
SB_HEADS = 32
SB_TQ = 512
SB_TK = 256
TOP_K = 2
LANES = 128
SUBLANES = 8
VMEM_LIMIT_BYTES = 56 * 1024 * 1024

MM_TM = 1024
MM_TN = 512
MOE_TM = 512
MOE_IN_TN = 512
MOE_OUT_TN = 512
GATHER_ROWS = 512
COMBINE_ROWS = 256


def _cparams(n_axes):
    return pltpu.CompilerParams(
        dimension_semantics=("arbitrary",) * n_axes,
        vmem_limit_bytes=VMEM_LIMIT_BYTES)


def _silu(x):
    return x / (1.0 + jnp.exp(-x))


def _log_sigmoid(x):
    return jnp.minimum(x, 0.0) - jnp.log1p(jnp.exp(-jnp.abs(x)))


def _rmsnorm_kernel(x_ref, g_ref, o_ref):
    x = x_ref[...]
    ms = jnp.mean(x * x, axis=-1, keepdims=True)
    o_ref[...] = (x * lax.rsqrt(ms + RMS_EPS) * g_ref[...]).astype(o_ref.dtype)


def _rmsnorm(x, g, out_dtype):
    m, d = x.shape
    tm = min(256, m)
    return pl.pallas_call(
        _rmsnorm_kernel,
        grid=(m // tm,),
        in_specs=[pl.BlockSpec((tm, d), lambda i: (i, 0)),
                  pl.BlockSpec((1, d), lambda i: (0, 0))],
        out_specs=pl.BlockSpec((tm, d), lambda i: (i, 0)),
        out_shape=jax.ShapeDtypeStruct((m, d), out_dtype),
        compiler_params=_cparams(1),
        name="rmsnorm",
    )(x, g.reshape(1, d))


def _mm_plain_kernel(x_ref, w_ref, o_ref):
    acc = jnp.dot(x_ref[...], w_ref[...].astype(BF16), preferred_element_type=F32)
    o_ref[...] = acc.astype(o_ref.dtype)


def _mm_nt_kernel(x_ref, wt_ref, o_ref):
    acc = lax.dot_general(x_ref[...], wt_ref[...].astype(BF16), (((1,), (1,)), ((), ())),
                          preferred_element_type=F32)
    o_ref[...] = acc.astype(o_ref.dtype)


def _matmul_nt(x, wt, *, n_cols, tm=None, tn=None, out_dtype=BF16, name="matmul_nt"):
    m, k = x.shape
    tm = min(tm or MM_TM, m)
    tn = min(tn or MM_TN, n_cols)
    assert m % tm == 0 and n_cols % tn == 0
    return pl.pallas_call(
        _mm_nt_kernel, grid=(m // tm, n_cols // tn),
        in_specs=[pl.BlockSpec((tm, k), lambda i, j: (i, 0)),
                  pl.BlockSpec((tn, k), lambda i, j: (j, 0))],
        out_specs=pl.BlockSpec((tm, tn), lambda i, j: (i, j)),
        out_shape=jax.ShapeDtypeStruct((m, n_cols), out_dtype),
        compiler_params=_cparams(2), name=name,
    )(x, wt)


def _mm_res_kernel(x_ref, w_ref, res_ref, o_ref):
    acc = jnp.dot(x_ref[...], w_ref[...].astype(BF16), preferred_element_type=F32)
    o_ref[...] = res_ref[...] + acc


def _mm_swiglu_kernel(x_ref, wg_ref, wu_ref, o_ref):
    x = x_ref[...]
    g = jnp.dot(x, wg_ref[...].astype(BF16), preferred_element_type=F32)
    u = jnp.dot(x, wu_ref[...].astype(BF16), preferred_element_type=F32)
    o_ref[...] = (_silu(g) * u).astype(o_ref.dtype)


def _matmul(x, w, *, n_cols, tk, k_blk=0, w_col_blk=0, tm=None, tn=None,
            out_dtype=BF16, res=None, name="matmul"):
    m = x.shape[0]
    tm = min(tm or MM_TM, m)
    tn = min(tn or MM_TN, n_cols)
    assert m % tm == 0 and n_cols % tn == 0
    grid = (m // tm, n_cols // tn)
    x_spec = pl.BlockSpec((tm, tk), lambda i, j: (i, k_blk))
    w_spec = pl.BlockSpec((tk, tn), lambda i, j: (k_blk, w_col_blk + j))
    o_spec = pl.BlockSpec((tm, tn), lambda i, j: (i, j))
    if res is None:
        kern, in_specs, args = _mm_plain_kernel, [x_spec, w_spec], (x, w)
    else:
        kern, in_specs, args = _mm_res_kernel, [x_spec, w_spec, o_spec], (x, w, res)
        out_dtype = F32
    return pl.pallas_call(
        kern, grid=grid, in_specs=in_specs, out_specs=o_spec,
        out_shape=jax.ShapeDtypeStruct((m, n_cols), out_dtype),
        compiler_params=_cparams(2), name=name,
    )(*args)


def _swiglu_in(x, w, d_ff, *, tm=None, tn=None, name="swiglu_in"):
    m, k = x.shape
    tm = min(tm or MM_TM, m)
    tn = tn or 256
    assert m % tm == 0 and d_ff % tn == 0
    nb = d_ff // tn
    return pl.pallas_call(
        _mm_swiglu_kernel,
        grid=(m // tm, nb),
        in_specs=[pl.BlockSpec((tm, k), lambda i, j: (i, 0)),
                  pl.BlockSpec((k, tn), lambda i, j: (0, j)),
                  pl.BlockSpec((k, tn), lambda i, j: (0, nb + j))],
        out_specs=pl.BlockSpec((tm, tn), lambda i, j: (i, j)),
        out_shape=jax.ShapeDtypeStruct((m, d_ff), BF16),
        compiler_params=_cparams(2), name=name,
    )(x, w, w)


def _dot_f32(a, b):
    return jnp.dot(a, b, preferred_element_type=F32, precision=lax.Precision.HIGHEST)


def _gla_gate_kernel(h_ref, wa_ref, wg_ref, bg_ref, o_ref, span_ref, *, chunk, rank):
    tm, d = h_ref.shape
    w_a = jnp.concatenate([wa_ref[...].astype(BF16), jnp.zeros((LANES - rank, d), BF16)], axis=0)
    a_low = lax.dot_general(h_ref[...], w_a, (((1,), (1,)), ((), ())),
                            preferred_element_type=F32)
    xg = _dot_f32(a_low, wg_ref[...]) + bg_ref[...]
    log_alpha = _log_sigmoid(xg) * (1.0 / GLA_GATE_TAU)
    row = lax.broadcasted_iota(jnp.int32, (tm, tm), 0)
    col = lax.broadcasted_iota(jnp.int32, (tm, tm), 1)
    same_chunk = (row // chunk) == (col // chunk)
    tril = jnp.where(same_chunk & (col <= row), 1.0, 0.0).astype(F32)
    b = _dot_f32(tril, log_alpha)
    o_ref[...] = b
    spans = [jnp.max(b[c * chunk:c * chunk + 1, :] - b[(c + 1) * chunk - 1:(c + 1) * chunk, :],
                     axis=-1, keepdims=True) for c in range(tm // chunk)]
    spans = jnp.concatenate(spans + [jnp.zeros((span_ref.shape[0] - len(spans), 1), F32)], axis=0)
    span_ref[...] = jnp.broadcast_to(spans, span_ref.shape)


def _gla_gate(h, w_in_t, n_proj, w_gate, b_gate, chunk):
    m, d = h.shape
    rank, dk = w_gate.shape
    assert n_proj % rank == 0 and rank % 16 == 0 and rank <= LANES
    tm = min(256, m)
    per_step = tm // chunk
    assert tm % chunk == 0 and per_step <= SUBLANES
    wg_pad = jnp.zeros((LANES, dk), F32).at[:rank, :].set(w_gate)
    b, spans = pl.pallas_call(
        functools.partial(_gla_gate_kernel, chunk=chunk, rank=rank),
        grid=(m // tm,),
        in_specs=[pl.BlockSpec((tm, d), lambda i: (i, 0)),
                  pl.BlockSpec((rank, d), lambda i: (n_proj // rank, 0)),
                  pl.BlockSpec((LANES, dk), lambda i: (0, 0)),
                  pl.BlockSpec((1, dk), lambda i: (0, 0))],
        out_specs=[pl.BlockSpec((tm, dk), lambda i: (i, 0)),
                   pl.BlockSpec((None, SUBLANES, LANES), lambda i: (i, 0, 0))],
        out_shape=[jax.ShapeDtypeStruct((m, dk), F32),
                   jax.ShapeDtypeStruct((m // tm, SUBLANES, LANES), F32)],
        compiler_params=_cparams(1), name="gla_gate",
    )(h, w_in_t, wg_pad, b_gate.reshape(1, dk))
    return b, spans[:, :per_step, 0].reshape(m // chunk)


def _gla_core_kernel(mild_ref, q_ref, k_ref, v_ref, r_ref, b_ref, gain_ref, o_ref, s_ref, sc_ref, *, scale, sub):
    chunk, hk = q_ref.shape
    mild_decay = mild_ref[pl.program_id(0) * pl.num_programs(2) + pl.program_id(2)] == 1

    @pl.when(pl.program_id(2) == 0)
    def _():
        s_ref[...] = jnp.zeros_like(s_ref)

    b = b_ref[...]
    q = q_ref[...].astype(F32) * scale
    k = k_ref[...].astype(F32)
    v = v_ref[...]
    state = s_ref[...]

    o = jnp.dot((q * jnp.exp(b)).astype(BF16), state.astype(BF16), preferred_element_type=F32)

    @pl.when(mild_decay)
    def _():
        b0 = b[0:1, :]
        q_t = (q * jnp.exp(b - b0)).astype(BF16)
        k_t = (k * jnp.exp(b0 - b)).astype(BF16)
        full = lax.dot_general(q_t, k_t, (((1,), (1,)), ((), ())), preferred_element_type=F32)
        t_id = lax.broadcasted_iota(jnp.int32, (chunk, chunk), 0)
        s_id = lax.broadcasted_iota(jnp.int32, (chunk, chunk), 1)
        sc_ref[...] = jnp.where(s_id <= t_id, full, 0.0)

    @pl.when(jnp.logical_not(mild_decay))
    def _():
        row_id = lax.broadcasted_iota(jnp.int32, (sub, chunk), 0)
        key_id = lax.broadcasted_iota(jnp.int32, (sub, chunk), 1)
        for i in range(chunk // sub):
            lo = i * sub
            b_i = b[lo:lo + sub, :]
            q_i = q[lo:lo + sub, :]
            diag = jnp.zeros((sub, chunk), F32)
            for j in range(sub):
                s = lo + j
                decay = jnp.exp(jnp.minimum(b_i - b[s:s + 1, :], 0.0))
                col = jnp.sum(q_i * decay * k[s:s + 1, :], axis=-1, keepdims=True)
                diag = jnp.where(key_id == s, col, diag)
            scores = jnp.where(key_id - lo <= row_id, diag, 0.0)
            if i > 0:
                b_first = b[lo:lo + 1, :]
                q_t = (q_i * jnp.exp(b_i - b_first)).astype(BF16)
                k_t = (k * jnp.exp(jnp.minimum(b_first - b, 0.0))).astype(BF16)
                below = lax.dot_general(q_t, k_t, (((1,), (1,)), ((), ())), preferred_element_type=F32)
                scores = jnp.where(key_id < lo, below, scores)
            sc_ref[lo:lo + sub, :] = scores

    o = o + jnp.dot(sc_ref[...].astype(BF16), v, preferred_element_type=F32)

    b_last = b[chunk - 1:chunk, :]
    k_state = (k * jnp.exp(b_last - b)).astype(BF16)
    update = lax.dot_general(k_state, v, (((0,), (0,)), ((), ())), preferred_element_type=F32)
    decay_rows = jnp.broadcast_to(jnp.exp(b_last), (LANES, hk))
    decay_col = jnp.transpose(decay_rows)[:, 0:1]
    s_ref[...] = state * decay_col + update

    ms = jnp.mean(o * o, axis=-1, keepdims=True)
    o = o * lax.rsqrt(ms + RMS_EPS) * gain_ref[...]
    o_ref[...] = (o * _silu(r_ref[...].astype(F32))).astype(o_ref.dtype)


def _gla_core(proj, b, chunk_decay, gain, batch, seq, heads, dk, dv):
    m = batch * seq
    hk, hv = dk // heads, dv // heads
    chunk = min(GLA_CHUNK, seq)
    nc = seq // chunk
    k_blk0 = dk // hk
    v_blk0 = (2 * dk) // hv
    r_blk0 = (2 * dk + dv) // hv
    mild = (chunk_decay < GLA_MILD_DECAY).astype(jnp.int32)

    def rows(bi, ci):
        return bi * nc + ci

    return pl.pallas_call(
        functools.partial(_gla_core_kernel, scale=float(hk) ** -0.5, sub=min(GLA_SUB, chunk)),
        grid_spec=pltpu.PrefetchScalarGridSpec(
            num_scalar_prefetch=1, grid=(batch, heads, nc),
            in_specs=[pl.BlockSpec((chunk, hk), lambda bi, h, c, mild: (rows(bi, c), h)),
                      pl.BlockSpec((chunk, hk), lambda bi, h, c, mild: (rows(bi, c), k_blk0 + h)),
                      pl.BlockSpec((chunk, hv), lambda bi, h, c, mild: (rows(bi, c), v_blk0 + h)),
                      pl.BlockSpec((chunk, hv), lambda bi, h, c, mild: (rows(bi, c), r_blk0 + h)),
                      pl.BlockSpec((chunk, hk), lambda bi, h, c, mild: (rows(bi, c), h)),
                      pl.BlockSpec((1, hv), lambda bi, h, c, mild: (0, 0))],
            out_specs=pl.BlockSpec((chunk, hv), lambda bi, h, c, mild: (rows(bi, c), h)),
            scratch_shapes=[pltpu.VMEM((hk, hv), F32), pltpu.VMEM((chunk, chunk), F32)]),
        out_shape=jax.ShapeDtypeStruct((m, dv), BF16),
        compiler_params=_cparams(3), name="gla_core",
    )(mild, proj, proj, proj, proj, b, gain.reshape(1, hv))


def _sb_kernel(q_ref, k_ref, v_ref, o_ref, acc_ref, carry_ref, hl0_ref, hl1_ref, lsp0_ref, lsp1_ref,
               sl0_ref, sl1_ref, tot0_ref, tot1_ref, *, scale, tk):
    tq, dh = q_ref.shape
    ratio = tq // tk
    assert ratio % 2 == 0
    hl_refs, lsp_refs = (hl0_ref, hl1_ref), (lsp0_ref, lsp1_ref)
    sl_refs, tot_refs = (sl0_ref, sl1_ref), (tot0_ref, tot1_ref)
    qi = pl.program_id(2)
    q = (q_ref[...].astype(F32) * (scale * LOG2_E)).astype(BF16)

    later = lax.broadcasted_iota(jnp.int32, (tk, tk), 0)
    key = lax.broadcasted_iota(jnp.int32, (tk, tk), 1)
    suffix_ones = jnp.where((later > key) | (key == tk - 1), 1.0, 0.0).astype(BF16)

    acc_ref[...] = jnp.zeros_like(acc_ref)
    carry_ref[...] = jnp.zeros_like(carry_ref)

    def logit_stage(k_start, parity, strict):
        k_blk = k_ref[pl.ds(k_start, tk), :]
        z = lax.dot_general(q, k_blk, (((1,), (1,)), ((), ())), preferred_element_type=F32)
        neg_abs = pltpu.bitcast(pltpu.bitcast(z, jnp.uint32) | jnp.uint32(0x80000000), F32)
        ls_pos = jnp.minimum(z, 0.0) - jnp.log(1.0 + jnp.exp2(neg_abs)) * LOG2_E
        log_1m = ls_pos - z
        if strict is not None:
            log_1m = jnp.where(strict, log_1m, 0.0)
        hl_refs[parity][...] = log_1m.astype(BF16)
        lsp_refs[parity][...] = ls_pos

    def suffix_stage(parity):
        sums = jnp.dot(hl_refs[parity][...], suffix_ones, preferred_element_type=F32)
        sl_refs[parity][...] = lsp_refs[parity][...] + jnp.where(lane == tk - 1, 0.0, sums)
        tot_refs[parity][...] = jnp.broadcast_to(sums[:, tk - 1:tk], (tq, LANES))

    def value_stage(k_start, parity, strict):
        v_blk = v_ref[pl.ds(k_start, tk), :]
        carry = carry_ref[...]
        w = jnp.exp2(sl_refs[parity][...] + jnp.concatenate([carry] * (tk // LANES), axis=1))
        if strict is not None:
            w = jnp.where(strict, w, 0.0)
        acc_ref[...] += jnp.dot(w.astype(BF16), v_blk, preferred_element_type=F32)
        carry_ref[...] = carry + tot_refs[parity][...]

    def k_start_of(t):
        return pl.multiple_of(qi * tq + (ratio - 1 - t) * tk, tk)

    row = lax.broadcasted_iota(jnp.int32, (tq, tk), 0)
    lane = lax.broadcasted_iota(jnp.int32, (tq, tk), 1)

    def strict_mask(t):
        return (lane + (ratio - 1 - t) * tk) < row if t < ratio else None

    def run_step(s, parity, mask_s, mask_ahead, do_logit=True, do_suffix=True):
        if do_logit:
            logit_stage(k_start_of(s + 2), parity, mask_ahead)
        if do_suffix:
            suffix_stage(1 - parity)
        value_stage(k_start_of(s), parity, mask_s)

    def prologue():
        logit_stage(k_start_of(0), 0, strict_mask(0))
        logit_stage(k_start_of(1), 1, strict_mask(1))
        suffix_stage(0)

    @pl.when(qi == 0)
    def _():
        prologue()
        for s in range(ratio):
            run_step(s, s % 2, strict_mask(s), strict_mask(s + 2),
                     do_logit=s + 2 < ratio, do_suffix=s + 1 < ratio)

    @pl.when(qi > 0)
    def _():
        n_tiles = ratio * (qi + 1)
        prologue()
        for s in range(ratio):
            run_step(s, s % 2, strict_mask(s), strict_mask(s + 2))

        def body(it, _):
            for j in range(2):
                run_step(ratio + 2 * it + j, j, None, None)
            return 0

        lax.fori_loop(0, (ratio * qi - 2) // 2, body, 0)
        run_step(n_tiles - 2, 0, None, None, do_logit=False)
        run_step(n_tiles - 1, 1, None, None, do_logit=False, do_suffix=False)

    o_ref[...] = acc_ref[...].astype(o_ref.dtype)


def _sb_core(qkv, batch, seq, heads, d_model):
    dh = d_model // heads
    tq = min(SB_TQ, seq)
    tk = min(SB_TK, tq)
    nq = seq // tq
    return pl.pallas_call(
        functools.partial(_sb_kernel, scale=float(dh) ** -0.5, tk=tk),
        grid=(batch, heads, nq),
        in_specs=[pl.BlockSpec((tq, dh), lambda b, h, i: (b * nq + i, h)),
                  pl.BlockSpec((seq, dh), lambda b, h, i: (b, heads + h)),
                  pl.BlockSpec((seq, dh), lambda b, h, i: (b, 2 * heads + h))],
        out_specs=pl.BlockSpec((tq, dh), lambda b, h, i: (b * nq + i, h)),
        out_shape=jax.ShapeDtypeStruct((batch * seq, d_model), BF16),
        scratch_shapes=[pltpu.VMEM((tq, dh), F32), pltpu.VMEM((tq, LANES), F32),
                        pltpu.VMEM((tq, tk), BF16), pltpu.VMEM((tq, tk), BF16),
                        pltpu.VMEM((tq, tk), F32), pltpu.VMEM((tq, tk), F32),
                        pltpu.VMEM((tq, tk), F32), pltpu.VMEM((tq, tk), F32),
                        pltpu.VMEM((tq, LANES), F32), pltpu.VMEM((tq, LANES), F32)],
        compiler_params=_cparams(3), name="sb_core",
    )(qkv, qkv, qkv)


def _router_kernel(x_ref, g_ref, wr_ref, h_ref, top_ref, *, n_experts):
    x = x_ref[...]
    ms = jnp.mean(x * x, axis=-1, keepdims=True)
    h = x * lax.rsqrt(ms + RMS_EPS) * g_ref[...]
    h_ref[...] = h
    logits = _dot_f32(h, wr_ref[...])
    lane = lax.broadcasted_iota(jnp.int32, logits.shape, 1)
    logits = jnp.where(lane < n_experts, logits, -jnp.inf)
    v1 = jnp.max(logits, axis=-1, keepdims=True)
    i1 = jnp.min(jnp.where(logits == v1, lane, LANES), axis=-1, keepdims=True)
    rest = jnp.where(lane == i1, -jnp.inf, logits)
    v2 = jnp.max(rest, axis=-1, keepdims=True)
    i2 = jnp.min(jnp.where(rest == v2, lane, LANES), axis=-1, keepdims=True)
    e = jnp.exp(v2 - v1)
    g1 = 1.0 / (1.0 + e)
    g2 = e * g1
    out = jnp.where(lane == 0, i1.astype(F32),
                    jnp.where(lane == 1, i2.astype(F32),
                              jnp.where(lane == 2, g1, jnp.where(lane == 3, g2, 0.0))))
    top_ref[...] = out


def _router(x, g, w_router):
    m, d = x.shape
    n_experts = w_router.shape[1]
    tm = min(256, m)
    wr_pad = jnp.zeros((d, LANES), F32).at[:, :n_experts].set(w_router)
    return pl.pallas_call(
        functools.partial(_router_kernel, n_experts=n_experts),
        grid=(m // tm,),
        in_specs=[pl.BlockSpec((tm, d), lambda i: (i, 0)),
                  pl.BlockSpec((1, d), lambda i: (0, 0)),
                  pl.BlockSpec((d, LANES), lambda i: (0, 0))],
        out_specs=[pl.BlockSpec((tm, d), lambda i: (i, 0)),
                   pl.BlockSpec((tm, LANES), lambda i: (i, 0))],
        out_shape=[jax.ShapeDtypeStruct((m, d), F32),
                   jax.ShapeDtypeStruct((m, LANES), F32)],
        compiler_params=_cparams(1), name="moe_router",
    )(x, g.reshape(1, d), wr_pad)


def _row_copy(src_hbm, dst_ref, sem, src_row, dst_row):
    return pltpu.make_async_copy(src_hbm.at[pl.ds(src_row, 1), :],
                                 dst_ref.at[pl.ds(dst_row, 1), :], sem)


def _gather_kernel(idx_ref, nrows_ref, src_hbm, o_ref, buf_ref, sem):
    rows = o_ref.shape[0]
    base = pl.program_id(0) * rows

    @pl.when(base < nrows_ref[0])
    def _():
        def start(r, _):
            _row_copy(src_hbm, buf_ref, sem, idx_ref[base + r], r).start()
            return 0

        def wait(r, _):
            _row_copy(src_hbm, buf_ref, sem, 0, r).wait()
            return 0

        lax.fori_loop(0, rows, start, 0)
        lax.fori_loop(0, rows, wait, 0)
        o_ref[...] = buf_ref[...].astype(o_ref.dtype)

    @pl.when(base >= nrows_ref[0])
    def _():
        o_ref[...] = jnp.zeros_like(o_ref)


def _gather_rows(src, idx, n_rows_valid, out_dtype):
    n = idx.shape[0]
    d = src.shape[1]
    rows = min(GATHER_ROWS, n)
    assert n % rows == 0
    return pl.pallas_call(
        _gather_kernel,
        grid_spec=pltpu.PrefetchScalarGridSpec(
            num_scalar_prefetch=2, grid=(n // rows,),
            in_specs=[pl.BlockSpec(memory_space=pl.ANY)],
            out_specs=pl.BlockSpec((rows, d), lambda i, idx, nr: (i, 0)),
            scratch_shapes=[pltpu.VMEM((rows, d), src.dtype), pltpu.SemaphoreType.DMA(())]),
        out_shape=jax.ShapeDtypeStruct((n, d), out_dtype),
        compiler_params=_cparams(1), name="moe_gather",
    )(idx, n_rows_valid, src)


def _weight_stream(plan_ref, copies, refill):
    j, i = pl.program_id(0), pl.program_id(1)

    @pl.when((j == 0) & (i == 0))
    def _():
        for cp in copies(plan_ref[0, 0], 0):
            cp.start()

    @pl.when(plan_ref[1, i] == 1)
    def _():
        for cp in copies(plan_ref[0, i], j):
            cp.wait()
        refill()
        j_next = j + plan_ref[3, i]

        @pl.when(j_next < pl.num_programs(0))
        def _():
            for cp in copies(plan_ref[2, i], j_next):
                cp.start()


def _moe_in_kernel(plan_ref, nvalid_ref, x_ref, w_hbm, o_ref, wf_ref, wb_ref, sem, *, d_ff, last_shift):
    j, i = pl.program_id(0), pl.program_id(1)
    tn = o_ref.shape[1]

    def copies(expert, jb):
        c = pl.multiple_of(jnp.minimum(jb * (tn // LANES), (d_ff - tn) // LANES) * LANES, LANES)
        return (pltpu.make_async_copy(w_hbm.at[expert, :, pl.ds(c, tn)], wf_ref.at[:, pl.ds(0, tn)], sem.at[0]),
                pltpu.make_async_copy(w_hbm.at[expert, :, pl.ds(pl.multiple_of(d_ff + c, LANES), tn)],
                                      wf_ref.at[:, pl.ds(tn, tn)], sem.at[1]))

    def refill():
        wb_ref[...] = wf_ref[...].astype(BF16)

    _weight_stream(plan_ref, copies, refill)

    @pl.when(i < nvalid_ref[0])
    def _():
        gu = jnp.dot(x_ref[...], wb_ref[...], preferred_element_type=F32)
        act = (_silu(gu[:, :tn]) * gu[:, tn:]).astype(o_ref.dtype)
        if last_shift == 0:
            o_ref[...] = act
        else:
            is_last = j == pl.num_programs(0) - 1

            @pl.when(is_last)
            def _():
                o_ref[:, :tn - last_shift] = act[:, last_shift:]
                o_ref[:, tn - last_shift:] = jnp.zeros((act.shape[0], last_shift), o_ref.dtype)

            @pl.when(jnp.logical_not(is_last))
            def _():
                o_ref[...] = act

    @pl.when(i >= nvalid_ref[0])
    def _():
        o_ref[...] = jnp.zeros_like(o_ref)


def _moe_in(xs, w_in, plan, n_valid, tm):
    r, d = xs.shape
    d_ff = w_in.shape[2] // 2
    tn = min(MOE_IN_TN, d_ff)
    nb = pl.cdiv(d_ff, tn)
    assert d_ff % LANES == 0 and tn % LANES == 0

    def x_map(j, i, plan, nv):
        return (jnp.minimum(i, nv[0] - 1), 0)

    return pl.pallas_call(
        functools.partial(_moe_in_kernel, d_ff=d_ff, last_shift=nb * tn - d_ff),
        grid_spec=pltpu.PrefetchScalarGridSpec(
            num_scalar_prefetch=2, grid=(nb, r // tm),
            in_specs=[pl.BlockSpec((tm, d), x_map),
                      pl.BlockSpec(memory_space=pl.ANY)],
            out_specs=pl.BlockSpec((tm, tn), lambda j, i, plan, nv: (i, j)),
            scratch_shapes=[pltpu.VMEM((d, 2 * tn), F32), pltpu.VMEM((d, 2 * tn), BF16),
                            pltpu.SemaphoreType.DMA((2,))]),
        out_shape=jax.ShapeDtypeStruct((r, nb * tn), BF16),
        compiler_params=_cparams(2), name="moe_in",
    )(plan, n_valid, xs, w_in)


def _moe_out_kernel(plan_ref, nvalid_ref, x_ref, w_hbm, o_ref, wf_ref, wb_ref, sem):
    i = pl.program_id(1)
    tn = o_ref.shape[1]

    def copies(expert, jb):
        c = pl.multiple_of(jb * tn, LANES)
        return (pltpu.make_async_copy(w_hbm.at[expert, :, pl.ds(c, tn)], wf_ref, sem.at[0]),)

    def refill():
        wb_ref[...] = wf_ref[...].astype(BF16)

    _weight_stream(plan_ref, copies, refill)

    @pl.when(i < nvalid_ref[0])
    def _():
        o_ref[...] = jnp.dot(x_ref[...], wb_ref[...], preferred_element_type=F32)

    @pl.when(i >= nvalid_ref[0])
    def _():
        o_ref[...] = jnp.zeros_like(o_ref)


def _moe_out(act, w_out, plan, n_valid, tm):
    r = act.shape[0]
    d_ff, d = w_out.shape[1:]
    tn = min(MOE_OUT_TN, d)
    assert d % tn == 0 and tn % LANES == 0

    def x_map(j, i, plan, nv):
        return (jnp.minimum(i, nv[0] - 1), 0)

    return pl.pallas_call(
        _moe_out_kernel,
        grid_spec=pltpu.PrefetchScalarGridSpec(
            num_scalar_prefetch=2, grid=(d // tn, r // tm),
            in_specs=[pl.BlockSpec((tm, d_ff), x_map),
                      pl.BlockSpec(memory_space=pl.ANY)],
            out_specs=pl.BlockSpec((tm, tn), lambda j, i, plan, nv: (i, j)),
            scratch_shapes=[pltpu.VMEM((d_ff, tn), F32), pltpu.VMEM((d_ff, tn), BF16),
                            pltpu.SemaphoreType.DMA((1,))]),
        out_shape=jax.ShapeDtypeStruct((r, d), F32),
        compiler_params=_cparams(2), name="moe_out",
    )(plan, n_valid, act, w_out)


def _combine_kernel(dest_ref, x_ref, y_hbm, top_ref, g_ref, o_ref, a_ref, b_ref, sem):
    rows = x_ref.shape[0]
    base = pl.program_id(0) * rows

    def start(r, _):
        _row_copy(y_hbm, a_ref, sem, dest_ref[2 * (base + r)], r).start()
        _row_copy(y_hbm, b_ref, sem, dest_ref[2 * (base + r) + 1], r).start()
        return 0

    def wait(r, _):
        _row_copy(y_hbm, a_ref, sem, 0, r).wait()
        _row_copy(y_hbm, b_ref, sem, 0, r).wait()
        return 0

    lax.fori_loop(0, rows, start, 0)
    lax.fori_loop(0, rows, wait, 0)
    top = top_ref[...]
    moe = a_ref[...] * top[:, TOP_K:TOP_K + 1] + b_ref[...] * top[:, TOP_K + 1:TOP_K + 2]
    x = x_ref[...] + moe
    ms = jnp.mean(x * x, axis=-1, keepdims=True)
    o_ref[...] = x * lax.rsqrt(ms + RMS_EPS) * g_ref[...]


def _combine_norm(x, y, dest, top, g):
    m, d = x.shape
    rows = min(COMBINE_ROWS, m)
    return pl.pallas_call(
        _combine_kernel,
        grid_spec=pltpu.PrefetchScalarGridSpec(
            num_scalar_prefetch=1, grid=(m // rows,),
            in_specs=[pl.BlockSpec((rows, d), lambda i, dest: (i, 0)),
                      pl.BlockSpec(memory_space=pl.ANY),
                      pl.BlockSpec((rows, LANES), lambda i, dest: (i, 0)),
                      pl.BlockSpec((1, d), lambda i, dest: (0, 0))],
            out_specs=pl.BlockSpec((rows, d), lambda i, dest: (i, 0)),
            scratch_shapes=[pltpu.VMEM((rows, d), F32), pltpu.VMEM((rows, d), F32),
                            pltpu.SemaphoreType.DMA(())]),
        out_shape=jax.ShapeDtypeStruct((m, d), F32),
        compiler_params=_cparams(1), name="moe_combine_norm",
    )(dest, x, y, top, g.reshape(1, d))


def _dispatch_plan(top, n_experts, tm):
    m = top.shape[0]
    n_pairs = m * TOP_K
    expert = top[:, :TOP_K].astype(jnp.int32).reshape(n_pairs)
    onehot = (expert[:, None] == jnp.arange(n_experts, dtype=jnp.int32)[None, :]).astype(jnp.int32)
    before = jnp.cumsum(onehot, axis=0) - onehot
    rank = jnp.sum(before * onehot, axis=1)
    counts = jnp.sum(onehot, axis=0)
    tiles = (counts + tm - 1) // tm
    tile_end = jnp.cumsum(tiles)
    group_start = (tile_end - tiles) * tm
    dest = group_start[expert] + rank
    n_tiles = n_pairs // tm + n_experts
    n_rows = n_tiles * tm
    src_token = jnp.zeros((n_rows,), jnp.int32).at[dest].set(jnp.arange(n_pairs, dtype=jnp.int32) // TOP_K)
    tile_id = jnp.arange(n_tiles, dtype=jnp.int32)
    tile_expert = jnp.minimum(jnp.sum((tile_end[None, :] <= tile_id[:, None]).astype(jnp.int32), axis=1),
                              n_experts - 1)
    n_valid = tile_end[-1:].astype(jnp.int32)
    prev_expert = jnp.concatenate([jnp.full((1,), -1, jnp.int32), tile_expert[:-1]])
    first = (tile_id < n_valid[0]) & (tile_expert != prev_expert)
    later_first = first[None, :] & (tile_id[None, :] > tile_id[:, None])
    next_first = jnp.min(jnp.where(later_first, tile_id[None, :], n_tiles), axis=1)
    is_last_run = next_first == n_tiles
    next_expert = jnp.where(is_last_run, tile_expert[0], tile_expert[jnp.minimum(next_first, n_tiles - 1)])
    plan = jnp.stack([tile_expert, first.astype(jnp.int32), next_expert, is_last_run.astype(jnp.int32)])
    return dest.astype(jnp.int32), src_token, plan.astype(jnp.int32), n_valid


def kernel(x, attn_norm, ffn_norm, gla_w_in, gla_w_gate, gla_b_gate, gla_onorm, gla_w_out,
           sb_w_in, sb_w_out, dense_w_in, dense_w_out, moe_router, moe_w_in, moe_w_out,
           final_norm):
    batch, seq, d = x.shape
    m = batch * seq
    x = x.reshape(m, d)

    rank, dk = gla_w_gate.shape[1:]
    hv = gla_onorm.shape[1]
    dv = GLA_HEADS * hv
    n_proj = 2 * dk + 2 * dv
    h = _rmsnorm(x, attn_norm[0], BF16)
    w_in_t = jnp.transpose(gla_w_in[0])
    proj = _matmul_nt(h, w_in_t, n_cols=n_proj, name="gla_in")
    b, chunk_decay = _gla_gate(h, w_in_t, n_proj, gla_w_gate[0], gla_b_gate[0], min(GLA_CHUNK, seq))
    o = _gla_core(proj, b, chunk_decay, gla_onorm[0], batch, seq, GLA_HEADS, dk, dv)
    x = _matmul(o, gla_w_out[0], n_cols=d, tk=dv, res=x, name="gla_out")

    d_ff = dense_w_out.shape[1]
    h = _rmsnorm(x, ffn_norm[0], BF16)
    act = _swiglu_in(h, dense_w_in[0], d_ff, name="dense_in")
    half = d_ff // 2
    x = _matmul(act, dense_w_out[0], n_cols=d, tk=half, k_blk=0, tn=256, res=x, name="dense_out0")
    x = _matmul(act, dense_w_out[0], n_cols=d, tk=half, k_blk=1, tn=256, res=x, name="dense_out1")

    h = _rmsnorm(x, attn_norm[1], BF16)
    qkv = _matmul(h, sb_w_in[0], n_cols=3 * d, tk=d, name="sb_in")
    o = _sb_core(qkv, batch, seq, SB_HEADS, d)
    x = _matmul(o, sb_w_out[0], n_cols=d, tk=d, res=x, name="sb_out")

    n_experts = moe_router.shape[2]
    tm = min(MOE_TM, m)
    h32, top = _router(x, ffn_norm[1], moe_router[0])
    dest, src_token, plan, n_valid = _dispatch_plan(top, n_experts, tm)
    xs = _gather_rows(h32, src_token, n_valid * tm, BF16)
    act = _moe_in(xs, moe_w_in[0], plan, n_valid, tm)
    y = _moe_out(act, moe_w_out[0], plan, n_valid, tm)
    out = _combine_norm(x, y, dest, top, final_norm)
    return out.reshape(batch, seq, d)
```

```python
import functools

import jax
import jax.numpy as jnp
from jax import lax
from jax.experimental import pallas as pl
from jax.experimental.pallas import tpu as pltpu

F32 = jnp.float32
BF16 = jnp.bfloat16

RMS_EPS = 1e-6
LOG2_E = 1.4426950408889634
GLA_HEADS = 4
GLA_GATE_TAU = 16.0
GLA_CHUNK = 256
GLA_SUB = 16
GLA_MILD_DECAY = 60.0
SB_HEADS = 32
SB_TQ = 512
SB_TK = 256
TOP_K = 2
LANES = 128
SUBLANES = 8
VMEM_LIMIT_BYTES = 56 * 1024 * 1024

MM_TM = 1024
MM_TN = 512
MOE_TM = 512
MOE_IN_TN = 512
MOE_OUT_TN = 512
GATHER_ROWS = 512
COMBINE_ROWS = 256


def _cparams(n_axes):
    return pltpu.CompilerParams(
        dimension_semantics=("arbitrary",) * n_axes,
        vmem_limit_bytes=VMEM_LIMIT_BYTES)


def _silu(x):
    return x / (1.0 + jnp.exp(-x))


def _log_sigmoid(x):
    return jnp.minimum(x, 0.0) - jnp.log1p(jnp.exp(-jnp.abs(x)))


def _rmsnorm_kernel(x_ref, g_ref, o_ref):
    x = x_ref[...]
    ms = jnp.mean(x * x, axis=-1, keepdims=True)
    o_ref[...] = (x * lax.rsqrt(ms + RMS_EPS) * g_ref[...]).astype(o_ref.dtype)


def _rmsnorm(x, g, out_dtype):
    m, d = x.shape
    tm = min(256, m)
    return pl.pallas_call(
        _rmsnorm_kernel,
        grid=(m // tm,),
        in_specs=[pl.BlockSpec((tm, d), lambda i: (i, 0)),
                  pl.BlockSpec((1, d), lambda i: (0, 0))],
        out_specs=pl.BlockSpec((tm, d), lambda i: (i, 0)),
        out_shape=jax.ShapeDtypeStruct((m, d), out_dtype),
        compiler_params=_cparams(1),
        name="rmsnorm",
    )(x, g.reshape(1, d))


def _mm_plain_kernel(x_ref, w_ref, o_ref):
    acc = jnp.dot(x_ref[...], w_ref[...].astype(BF16), preferred_element_type=F32)
    o_ref[...] = acc.astype(o_ref.dtype)


def _mm_nt_kernel(x_ref, wt_ref, o_ref):
    acc = lax.dot_general(x_ref[...], wt_ref[...].astype(BF16), (((1,), (1,)), ((), ())),
                          preferred_element_type=F32)
    o_ref[...] = acc.astype(o_ref.dtype)


def _matmul_nt(x, wt, *, n_cols, tm=None, tn=None, out_dtype=BF16, name="matmul_nt"):
    m, k = x.shape
    tm = min(tm or MM_TM, m)
    tn = min(tn or MM_TN, n_cols)
    assert m % tm == 0 and n_cols % tn == 0
    return pl.pallas_call(
        _mm_nt_kernel, grid=(m // tm, n_cols // tn),
        in_specs=[pl.BlockSpec((tm, k), lambda i, j: (i, 0)),
                  pl.BlockSpec((tn, k), lambda i, j: (j, 0))],
        out_specs=pl.BlockSpec((tm, tn), lambda i, j: (i, j)),
        out_shape=jax.ShapeDtypeStruct((m, n_cols), out_dtype),
        compiler_params=_cparams(2), name=name,
    )(x, wt)


def _mm_res_kernel(x_ref, w_ref, res_ref, o_ref):
    acc = jnp.dot(x_ref[...], w_ref[...].astype(BF16), preferred_element_type=F32)
    o_ref[...] = res_ref[...] + acc


def _mm_swiglu_kernel(x_ref, wg_ref, wu_ref, o_ref):
    x = x_ref[...]
    g = jnp.dot(x, wg_ref[...].astype(BF16), preferred_element_type=F32)
    u = jnp.dot(x, wu_ref[...].astype(BF16), preferred_element_type=F32)
    o_ref[...] = (_silu(g) * u).astype(o_ref.dtype)


def _matmul(x, w, *, n_cols, tk, k_blk=0, w_col_blk=0, tm=None, tn=None,
            out_dtype=BF16, res=None, name="matmul"):
    m = x.shape[0]
    tm = min(tm or MM_TM, m)
    tn = min(tn or MM_TN, n_cols)
    assert m % tm == 0 and n_cols % tn == 0
    grid = (m // tm, n_cols // tn)
    x_spec = pl.BlockSpec((tm, tk), lambda i, j: (i, k_blk))
    w_spec = pl.BlockSpec((tk, tn), lambda i, j: (k_blk, w_col_blk + j))
    o_spec = pl.BlockSpec((tm, tn), lambda i, j: (i, j))
    if res is None:
        kern, in_specs, args = _mm_plain_kernel, [x_spec, w_spec], (x, w)
    else:
        kern, in_specs, args = _mm_res_kernel, [x_spec, w_spec, o_spec], (x, w, res)
        out_dtype = F32
    return pl.pallas_call(
        kern, grid=grid, in_specs=in_specs, out_specs=o_spec,
        out_shape=jax.ShapeDtypeStruct((m, n_cols), out_dtype),
        compiler_params=_cparams(2), name=name,
    )(*args)


def _swiglu_in(x, w, d_ff, *, tm=None, tn=None, name="swiglu_in"):
    m, k = x.shape
    tm = min(tm or MM_TM, m)
    tn = tn or 256
    assert m % tm == 0 and d_ff % tn == 0
    nb = d_ff // tn
    return pl.pallas_call(
        _mm_swiglu_kernel,
        grid=(m // tm, nb),
        in_specs=[pl.BlockSpec((tm, k), lambda i, j: (i, 0)),
                  pl.BlockSpec((k, tn), lambda i, j: (0, j)),
                  pl.BlockSpec((k, tn), lambda i, j: (0, nb + j))],
        out_specs=pl.BlockSpec((tm, tn), lambda i, j: (i, j)),
        out_shape=jax.ShapeDtypeStruct((m, d_ff), BF16),
        compiler_params=_cparams(2), name=name,
    )(x, w, w)


def _dot_f32(a, b):
    a_hi, a_lo = _split_bf16(a, 2)
    b_hi, b_lo = _split_bf16(b, 2)
    return (jnp.dot(a_hi, b_hi, preferred_element_type=F32) + jnp.dot(a_hi, b_lo, preferred_element_type=F32)
            + jnp.dot(a_lo, b_hi, preferred_element_type=F32))


def _split_bf16(x, terms):
    out = []
    for _ in range(terms):
        t = x.astype(BF16)
        out.append(t)
        x = x - t.astype(F32)
    return out


def _dot_exact_lhs(lhs, x):
    return sum(jnp.dot(lhs, t, preferred_element_type=F32) for t in _split_bf16(x, 3))


def _gla_gate_kernel(h_ref, wa_ref, wg_ref, bg_ref, o_ref, span_ref, *, chunk, rank):
    tm, d = h_ref.shape
    w_a = jnp.concatenate([wa_ref[...].astype(BF16), jnp.zeros((LANES - rank, d), BF16)], axis=0)
    a_low = lax.dot_general(h_ref[...], w_a, (((1,), (1,)), ((), ())),
                            preferred_element_type=F32)
    xg = _dot_f32(a_low, wg_ref[...]) + bg_ref[...]
    log_alpha = _log_sigmoid(xg) * (1.0 / GLA_GATE_TAU)
    row = lax.broadcasted_iota(jnp.int32, (tm, tm), 0)
    col = lax.broadcasted_iota(jnp.int32, (tm, tm), 1)
    same_chunk = (row // chunk) == (col // chunk)
    tril = jnp.where(same_chunk & (col <= row), 1.0, 0.0).astype(BF16)
    b = _dot_exact_lhs(tril, log_alpha)
    o_ref[...] = b
    spans = [jnp.max(b[c * chunk:c * chunk + 1, :] - b[(c + 1) * chunk - 1:(c + 1) * chunk, :],
                     axis=-1, keepdims=True) for c in range(tm // chunk)]
    spans = jnp.concatenate(spans + [jnp.zeros((span_ref.shape[0] - len(spans), 1), F32)], axis=0)
    span_ref[...] = jnp.broadcast_to(spans, span_ref.shape)


def _gla_gate(h, w_in_t, n_proj, w_gate, b_gate, chunk):
    m, d = h.shape
    rank, dk = w_gate.shape
    assert n_proj % rank == 0 and rank % 16 == 0 and rank <= LANES
    tm = min(256, m)
    per_step = tm // chunk
    assert tm % chunk == 0 and per_step <= SUBLANES
    wg_pad = jnp.zeros((LANES, dk), F32).at[:rank, :].set(w_gate)
    b, spans = pl.pallas_call(
        functools.partial(_gla_gate_kernel, chunk=chunk, rank=rank),
        grid=(m // tm,),
        in_specs=[pl.BlockSpec((tm, d), lambda i: (i, 0)),
                  pl.BlockSpec((rank, d), lambda i: (n_proj // rank, 0)),
                  pl.BlockSpec((LANES, dk), lambda i: (0, 0)),
                  pl.BlockSpec((1, dk), lambda i: (0, 0))],
        out_specs=[pl.BlockSpec((tm, dk), lambda i: (i, 0)),
                   pl.BlockSpec((None, SUBLANES, LANES), lambda i: (i, 0, 0))],
        out_shape=[jax.ShapeDtypeStruct((m, dk), F32),
                   jax.ShapeDtypeStruct((m // tm, SUBLANES, LANES), F32)],
        compiler_params=_cparams(1), name="gla_gate",
    )(h, w_in_t, wg_pad, b_gate.reshape(1, dk))
    return b, spans[:, :per_step, 0].reshape(m // chunk)


def _gla_core_kernel(mild_ref, q_ref, k_ref, v_ref, r_ref, b_ref, gain_ref, o_ref, s_ref, sc_ref, *, scale, sub):
    chunk, hk = q_ref.shape
    mild_decay = mild_ref[pl.program_id(0) * pl.num_programs(2) + pl.program_id(2)] == 1

    @pl.when(pl.program_id(2) == 0)
    def _():
        s_ref[...] = jnp.zeros_like(s_ref)

    b = b_ref[...]
    q = q_ref[...].astype(F32) * scale
    k = k_ref[...].astype(F32)
    v = v_ref[...]
    state = s_ref[...]

    o = jnp.dot((q * jnp.exp(b)).astype(BF16), state.astype(BF16), preferred_element_type=F32)

    @pl.when(mild_decay)
    def _():
        b0 = b[0:1, :]
        q_t = (q * jnp.exp(b - b0)).astype(BF16)
        k_t = (k * jnp.exp(b0 - b)).astype(BF16)
        full = lax.dot_general(q_t, k_t, (((1,), (1,)), ((), ())), preferred_element_type=F32)
        t_id = lax.broadcasted_iota(jnp.int32, (chunk, chunk), 0)
        s_id = lax.broadcasted_iota(jnp.int32, (chunk, chunk), 1)
        sc_ref[...] = jnp.where(s_id <= t_id, full, 0.0)

    @pl.when(jnp.logical_not(mild_decay))
    def _():
        row_id = lax.broadcasted_iota(jnp.int32, (sub, chunk), 0)
        key_id = lax.broadcasted_iota(jnp.int32, (sub, chunk), 1)
        for i in range(chunk // sub):
            lo = i * sub
            b_i = b[lo:lo + sub, :]
            q_i = q[lo:lo + sub, :]
            diag = jnp.zeros((sub, chunk), F32)
            for j in range(sub):
                s = lo + j
                decay = jnp.exp(jnp.minimum(b_i - b[s:s + 1, :], 0.0))
                col = jnp.sum(q_i * decay * k[s:s + 1, :], axis=-1, keepdims=True)
                diag = jnp.where(key_id == s, col, diag)
            scores = jnp.where(key_id - lo <= row_id, diag, 0.0)
            if i > 0:
                b_first = b[lo:lo + 1, :]
                q_t = (q_i * jnp.exp(b_i - b_first)).astype(BF16)
                k_t = (k * jnp.exp(jnp.minimum(b_first - b, 0.0))).astype(BF16)
                below = lax.dot_general(q_t, k_t, (((1,), (1,)), ((), ())), preferred_element_type=F32)
                scores = jnp.where(key_id < lo, below, scores)
            sc_ref[lo:lo + sub, :] = scores

    o = o + jnp.dot(sc_ref[...].astype(BF16), v, preferred_element_type=F32)

    b_last = b[chunk - 1:chunk, :]
    k_state = (k * jnp.exp(b_last - b)).astype(BF16)
    update = lax.dot_general(k_state, v, (((0,), (0,)), ((), ())), preferred_element_type=F32)
    decay_rows = jnp.broadcast_to(jnp.exp(b_last), (LANES, hk))
    decay_col = jnp.transpose(decay_rows)[:, 0:1]
    s_ref[...] = state * decay_col + update

    ms = jnp.mean(o * o, axis=-1, keepdims=True)
    o = o * lax.rsqrt(ms + RMS_EPS) * gain_ref[...]
    o_ref[...] = (o * _silu(r_ref[...].astype(F32))).astype(o_ref.dtype)


def _gla_core(proj, b, chunk_decay, gain, batch, seq, heads, dk, dv):
    m = batch * seq
    hk, hv = dk // heads, dv // heads
    chunk = min(GLA_CHUNK, seq)
    nc = seq // chunk
    k_blk0 = dk // hk
    v_blk0 = (2 * dk) // hv
    r_blk0 = (2 * dk + dv) // hv
    mild = (chunk_decay < GLA_MILD_DECAY).astype(jnp.int32)

    def rows(bi, ci):
        return bi * nc + ci

    return pl.pallas_call(
        functools.partial(_gla_core_kernel, scale=float(hk) ** -0.5, sub=min(GLA_SUB, chunk)),
        grid_spec=pltpu.PrefetchScalarGridSpec(
            num_scalar_prefetch=1, grid=(batch, heads, nc),
            in_specs=[pl.BlockSpec((chunk, hk), lambda bi, h, c, mild: (rows(bi, c), h)),
                      pl.BlockSpec((chunk, hk), lambda bi, h, c, mild: (rows(bi, c), k_blk0 + h)),
                      pl.BlockSpec((chunk, hv), lambda bi, h, c, mild: (rows(bi, c), v_blk0 + h)),
                      pl.BlockSpec((chunk, hv), lambda bi, h, c, mild: (rows(bi, c), r_blk0 + h)),
                      pl.BlockSpec((chunk, hk), lambda bi, h, c, mild: (rows(bi, c), h)),
                      pl.BlockSpec((1, hv), lambda bi, h, c, mild: (0, 0))],
            out_specs=pl.BlockSpec((chunk, hv), lambda bi, h, c, mild: (rows(bi, c), h)),
            scratch_shapes=[pltpu.VMEM((hk, hv), F32), pltpu.VMEM((chunk, chunk), F32)]),
        out_shape=jax.ShapeDtypeStruct((m, dv), BF16),
        compiler_params=_cparams(3), name="gla_core",
    )(mild, proj, proj, proj, proj, b, gain.reshape(1, hv))


def _sb_kernel(q_ref, k_ref, v_ref, o_ref, acc_ref, carry_ref, hl0_ref, hl1_ref, lsp0_ref, lsp1_ref,
               sl0_ref, sl1_ref, tot0_ref, tot1_ref, *, scale, tk):
    tq, dh = q_ref.shape
    ratio = tq // tk
    assert ratio % 2 == 0
    hl_refs, lsp_refs = (hl0_ref, hl1_ref), (lsp0_ref, lsp1_ref)
    sl_refs, tot_refs = (sl0_ref, sl1_ref), (tot0_ref, tot1_ref)
    qi = pl.program_id(2)
    q = (q_ref[...].astype(F32) * (scale * LOG2_E)).astype(BF16)

    later = lax.broadcasted_iota(jnp.int32, (tk, tk), 0)
    key = lax.broadcasted_iota(jnp.int32, (tk, tk), 1)
    suffix_ones = jnp.where((later > key) | (key == tk - 1), 1.0, 0.0).astype(BF16)

    acc_ref[...] = jnp.zeros_like(acc_ref)
    carry_ref[...] = jnp.zeros_like(carry_ref)

    def logit_stage(k_start, parity, strict):
        k_blk = k_ref[pl.ds(k_start, tk), :]
        z = lax.dot_general(q, k_blk, (((1,), (1,)), ((), ())), preferred_element_type=F32)
        neg_abs = pltpu.bitcast(pltpu.bitcast(z, jnp.uint32) | jnp.uint32(0x80000000), F32)
        ls_pos = jnp.minimum(z, 0.0) - jnp.log(1.0 + jnp.exp2(neg_abs)) * LOG2_E
        log_1m = ls_pos - z
        if strict is not None:
            log_1m = jnp.where(strict, log_1m, 0.0)
        hl_refs[parity][...] = log_1m.astype(BF16)
        lsp_refs[parity][...] = ls_pos

    def suffix_stage(parity):
        sums = jnp.dot(hl_refs[parity][...], suffix_ones, preferred_element_type=F32)
        sl_refs[parity][...] = lsp_refs[parity][...] + jnp.where(lane == tk - 1, 0.0, sums)
        tot_refs[parity][...] = jnp.broadcast_to(sums[:, tk - 1:tk], (tq, LANES))

    def value_stage(k_start, parity, strict):
        v_blk = v_ref[pl.ds(k_start, tk), :]
        carry = carry_ref[...]
        w = jnp.exp2(sl_refs[parity][...] + jnp.concatenate([carry] * (tk // LANES), axis=1))
        if strict is not None:
            w = jnp.where(strict, w, 0.0)
        acc_ref[...] += jnp.dot(w.astype(BF16), v_blk, preferred_element_type=F32)
        carry_ref[...] = carry + tot_refs[parity][...]

    def k_start_of(t):
        return pl.multiple_of(qi * tq + (ratio - 1 - t) * tk, tk)

    row = lax.broadcasted_iota(jnp.int32, (tq, tk), 0)
    lane = lax.broadcasted_iota(jnp.int32, (tq, tk), 1)

    def strict_mask(t):
        return (lane + (ratio - 1 - t) * tk) < row if t < ratio else None

    def run_step(s, parity, mask_s, mask_ahead, do_logit=True, do_suffix=True):
        if do_logit:
            logit_stage(k_start_of(s + 2), parity, mask_ahead)
        if do_suffix:
            suffix_stage(1 - parity)
        value_stage(k_start_of(s), parity, mask_s)

    def prologue():
        logit_stage(k_start_of(0), 0, strict_mask(0))
        logit_stage(k_start_of(1), 1, strict_mask(1))
        suffix_stage(0)

    @pl.when(qi == 0)
    def _():
        prologue()
        for s in range(ratio):
            run_step(s, s % 2, strict_mask(s), strict_mask(s + 2),
                     do_logit=s + 2 < ratio, do_suffix=s + 1 < ratio)

    @pl.when(qi > 0)
    def _():
        n_tiles = ratio * (qi + 1)
        prologue()
        for s in range(ratio):
            run_step(s, s % 2, strict_mask(s), strict_mask(s + 2))

        def body(it, _):
            for j in range(2):
                run_step(ratio + 2 * it + j, j, None, None)
            return 0

        lax.fori_loop(0, (ratio * qi - 2) // 2, body, 0)
        run_step(n_tiles - 2, 0, None, None, do_logit=False)
        run_step(n_tiles - 1, 1, None, None, do_logit=False, do_suffix=False)

    o_ref[...] = acc_ref[...].astype(o_ref.dtype)


def _sb_core(qkv, batch, seq, heads, d_model):
    dh = d_model // heads
    tq = min(SB_TQ, seq)
    tk = min(SB_TK, tq)
    nq = seq // tq
    return pl.pallas_call(
        functools.partial(_sb_kernel, scale=float(dh) ** -0.5, tk=tk),
        grid=(batch, heads, nq),
        in_specs=[pl.BlockSpec((tq, dh), lambda b, h, i: (b * nq + i, h)),
                  pl.BlockSpec((seq, dh), lambda b, h, i: (b, heads + h)),
                  pl.BlockSpec((seq, dh), lambda b, h, i: (b, 2 * heads + h))],
        out_specs=pl.BlockSpec((tq, dh), lambda b, h, i: (b * nq + i, h)),
        out_shape=jax.ShapeDtypeStruct((batch * seq, d_model), BF16),
        scratch_shapes=[pltpu.VMEM((tq, dh), F32), pltpu.VMEM((tq, LANES), F32),
                        pltpu.VMEM((tq, tk), BF16), pltpu.VMEM((tq, tk), BF16),
                        pltpu.VMEM((tq, tk), F32), pltpu.VMEM((tq, tk), F32),
                        pltpu.VMEM((tq, tk), F32), pltpu.VMEM((tq, tk), F32),
                        pltpu.VMEM((tq, LANES), F32), pltpu.VMEM((tq, LANES), F32)],
        compiler_params=_cparams(3), name="sb_core",
    )(qkv, qkv, qkv)


def _pack_bf16_pair(hi, lo):
    hi_bits = pltpu.bitcast(hi.astype(BF16).astype(F32), jnp.uint32)
    lo_bits = pltpu.bitcast(lo.astype(BF16).astype(F32), jnp.uint32)
    return hi_bits | (lo_bits >> jnp.uint32(16))


def _unpack_bf16_pair(words):
    hi = pltpu.bitcast(words & jnp.uint32(0xFFFF0000), F32)
    lo = pltpu.bitcast(words << jnp.uint32(16), F32)
    return hi, lo


def _router_kernel(x_ref, g_ref, wr_ref, h_ref, top_ref, *, n_experts):
    x = x_ref[...]
    half = x.shape[1] // 2
    ms = jnp.mean(x * x, axis=-1, keepdims=True)
    h = x * lax.rsqrt(ms + RMS_EPS) * g_ref[...]
    h_ref[...] = _pack_bf16_pair(h[:, :half], h[:, half:])
    logits = _dot_f32(h, wr_ref[...])
    lane = lax.broadcasted_iota(jnp.int32, logits.shape, 1)
    logits = jnp.where(lane < n_experts, logits, -jnp.inf)
    v1 = jnp.max(logits, axis=-1, keepdims=True)
    i1 = jnp.min(jnp.where(logits == v1, lane, LANES), axis=-1, keepdims=True)
    rest = jnp.where(lane == i1, -jnp.inf, logits)
    v2 = jnp.max(rest, axis=-1, keepdims=True)
    i2 = jnp.min(jnp.where(rest == v2, lane, LANES), axis=-1, keepdims=True)
    e = jnp.exp(v2 - v1)
    g1 = 1.0 / (1.0 + e)
    g2 = e * g1
    out = jnp.where(lane == 0, i1.astype(F32),
                    jnp.where(lane == 1, i2.astype(F32),
                              jnp.where(lane == 2, g1, jnp.where(lane == 3, g2, 0.0))))
    top_ref[...] = out


def _router(x, g, w_router):
    m, d = x.shape
    n_experts = w_router.shape[1]
    tm = min(256, m)
    wr_pad = jnp.zeros((d, LANES), F32).at[:, :n_experts].set(w_router)
    return pl.pallas_call(
        functools.partial(_router_kernel, n_experts=n_experts),
        grid=(m // tm,),
        in_specs=[pl.BlockSpec((tm, d), lambda i: (i, 0)),
                  pl.BlockSpec((1, d), lambda i: (0, 0)),
                  pl.BlockSpec((d, LANES), lambda i: (0, 0))],
        out_specs=[pl.BlockSpec((tm, d // 2), lambda i: (i, 0)),
                   pl.BlockSpec((tm, LANES), lambda i: (i, 0))],
        out_shape=[jax.ShapeDtypeStruct((m, d // 2), jnp.uint32),
                   jax.ShapeDtypeStruct((m, LANES), F32)],
        compiler_params=_cparams(1), name="moe_router",
    )(x, g.reshape(1, d), wr_pad)


def _row_copy(src_hbm, dst_ref, sem, src_row, dst_row):
    return pltpu.make_async_copy(src_hbm.at[pl.ds(src_row, 1), :],
                                 dst_ref.at[pl.ds(dst_row, 1), :], sem)


def _gather_kernel(idx_ref, nrows_ref, src_hbm, o_ref, buf_ref, sem):
    rows = o_ref.shape[0]
    base = pl.program_id(0) * rows

    @pl.when(base < nrows_ref[0])
    def _():
        def start(r, _):
            _row_copy(src_hbm, buf_ref, sem, idx_ref[base + r], r).start()
            return 0

        def wait(r, _):
            _row_copy(src_hbm, buf_ref, sem, 0, r).wait()
            return 0

        lax.fori_loop(0, rows, start, 0)
        lax.fori_loop(0, rows, wait, 0)
        half = buf_ref.shape[1]
        hi, lo = _unpack_bf16_pair(buf_ref[...])
        o_ref[:, :half] = hi.astype(o_ref.dtype)
        o_ref[:, half:] = lo.astype(o_ref.dtype)

    @pl.when(base >= nrows_ref[0])
    def _():
        o_ref[...] = jnp.zeros_like(o_ref)


def _gather_rows(src_packed, idx, n_rows_valid):
    n = idx.shape[0]
    half = src_packed.shape[1]
    rows = min(GATHER_ROWS, n)
    assert n % rows == 0
    return pl.pallas_call(
        _gather_kernel,
        grid_spec=pltpu.PrefetchScalarGridSpec(
            num_scalar_prefetch=2, grid=(n // rows,),
            in_specs=[pl.BlockSpec(memory_space=pl.ANY)],
            out_specs=pl.BlockSpec((rows, 2 * half), lambda i, idx, nr: (i, 0)),
            scratch_shapes=[pltpu.VMEM((rows, half), jnp.uint32), pltpu.SemaphoreType.DMA(())]),
        out_shape=jax.ShapeDtypeStruct((n, 2 * half), BF16),
        compiler_params=_cparams(1), name="moe_gather",
    )(idx, n_rows_valid, src_packed)


def _weight_stream(plan_ref, copies, refill):
    j, i = pl.program_id(0), pl.program_id(1)

    @pl.when((j == 0) & (i == 0))
    def _():
        for cp in copies(plan_ref[0, 0], 0):
            cp.start()

    @pl.when(plan_ref[1, i] == 1)
    def _():
        for cp in copies(plan_ref[0, i], j):
            cp.wait()
        refill()
        j_next = j + plan_ref[3, i]

        @pl.when(j_next < pl.num_programs(0))
        def _():
            for cp in copies(plan_ref[2, i], j_next):
                cp.start()


def _moe_in_kernel(plan_ref, nvalid_ref, x_ref, w_hbm, o_ref, wf_ref, wb_ref, sem, *, d_ff, last_shift):
    j, i = pl.program_id(0), pl.program_id(1)
    tn = o_ref.shape[1]

    def copies(expert, jb):
        c = pl.multiple_of(jnp.minimum(jb * (tn // LANES), (d_ff - tn) // LANES) * LANES, LANES)
        return (pltpu.make_async_copy(w_hbm.at[expert, :, pl.ds(c, tn)], wf_ref.at[:, pl.ds(0, tn)], sem.at[0]),
                pltpu.make_async_copy(w_hbm.at[expert, :, pl.ds(pl.multiple_of(d_ff + c, LANES), tn)],
                                      wf_ref.at[:, pl.ds(tn, tn)], sem.at[1]))

    def refill():
        wb_ref[...] = wf_ref[...].astype(BF16)

    _weight_stream(plan_ref, copies, refill)

    @pl.when(i < nvalid_ref[0])
    def _():
        gu = jnp.dot(x_ref[...], wb_ref[...], preferred_element_type=F32)
        act = (_silu(gu[:, :tn]) * gu[:, tn:]).astype(o_ref.dtype)
        if last_shift == 0:
            o_ref[...] = act
        else:
            is_last = j == pl.num_programs(0) - 1

            @pl.when(is_last)
            def _():
                o_ref[:, :tn - last_shift] = act[:, last_shift:]
                o_ref[:, tn - last_shift:] = jnp.zeros((act.shape[0], last_shift), o_ref.dtype)

            @pl.when(jnp.logical_not(is_last))
            def _():
                o_ref[...] = act

    @pl.when(i >= nvalid_ref[0])
    def _():
        o_ref[...] = jnp.zeros_like(o_ref)


def _moe_in(xs, w_in, plan, n_valid, tm):
    r, d = xs.shape
    d_ff = w_in.shape[2] // 2
    tn = min(MOE_IN_TN, d_ff)
    nb = pl.cdiv(d_ff, tn)
    assert d_ff % LANES == 0 and tn % LANES == 0

    def x_map(j, i, plan, nv):
        return (jnp.minimum(i, nv[0] - 1), 0)

    return pl.pallas_call(
        functools.partial(_moe_in_kernel, d_ff=d_ff, last_shift=nb * tn - d_ff),
        grid_spec=pltpu.PrefetchScalarGridSpec(
            num_scalar_prefetch=2, grid=(nb, r // tm),
            in_specs=[pl.BlockSpec((tm, d), x_map),
                      pl.BlockSpec(memory_space=pl.ANY)],
            out_specs=pl.BlockSpec((tm, tn), lambda j, i, plan, nv: (i, j)),
            scratch_shapes=[pltpu.VMEM((d, 2 * tn), F32), pltpu.VMEM((d, 2 * tn), BF16),
                            pltpu.SemaphoreType.DMA((2,))]),
        out_shape=jax.ShapeDtypeStruct((r, nb * tn), BF16),
        compiler_params=_cparams(2), name="moe_in",
    )(plan, n_valid, xs, w_in)


def _moe_out_kernel(plan_ref, nvalid_ref, x_ref, w_hbm, o_ref, wf_ref, wb_ref, sem):
    i = pl.program_id(1)
    tn = wf_ref.shape[1]

    def copies(expert, jb):
        c = pl.multiple_of(jb * tn, LANES)
        return (pltpu.make_async_copy(w_hbm.at[expert, :, pl.ds(c, tn)], wf_ref, sem.at[0]),)

    def refill():
        wb_ref[...] = wf_ref[...].astype(BF16)

    _weight_stream(plan_ref, copies, refill)

    @pl.when(i < nvalid_ref[0])
    def _():
        y = jnp.dot(x_ref[...], wb_ref[...], preferred_element_type=F32)
        o_ref[...] = _pack_bf16_pair(y[:, :tn // 2], y[:, tn // 2:])

    @pl.when(i >= nvalid_ref[0])
    def _():
        o_ref[...] = jnp.zeros_like(o_ref)


def _moe_out(act, w_out, plan, n_valid, tm):
    r = act.shape[0]
    d_ff, d = w_out.shape[1:]
    tn = min(MOE_OUT_TN, d)
    assert d % tn == 0 and tn % (2 * LANES) == 0

    def x_map(j, i, plan, nv):
        return (jnp.minimum(i, nv[0] - 1), 0)

    return pl.pallas_call(
        _moe_out_kernel,
        grid_spec=pltpu.PrefetchScalarGridSpec(
            num_scalar_prefetch=2, grid=(d // tn, r // tm),
            in_specs=[pl.BlockSpec((tm, d_ff), x_map),
                      pl.BlockSpec(memory_space=pl.ANY)],
            out_specs=pl.BlockSpec((tm, tn // 2), lambda j, i, plan, nv: (i, j)),
            scratch_shapes=[pltpu.VMEM((d_ff, tn), F32), pltpu.VMEM((d_ff, tn), BF16),
                            pltpu.SemaphoreType.DMA((1,))]),
        out_shape=jax.ShapeDtypeStruct((r, d // 2), jnp.uint32),
        compiler_params=_cparams(2), name="moe_out",
    )(plan, n_valid, act, w_out)


def _combine_kernel(dest_ref, x_ref, y_hbm, top_ref, g_ref, o_ref, a_ref, b_ref, sem, *, group):
    rows = x_ref.shape[0]
    base = pl.program_id(0) * rows

    def start(r, _):
        _row_copy(y_hbm, a_ref, sem, dest_ref[2 * (base + r)], r).start()
        _row_copy(y_hbm, b_ref, sem, dest_ref[2 * (base + r) + 1], r).start()
        return 0

    def wait(r, _):
        _row_copy(y_hbm, a_ref, sem, 0, r).wait()
        _row_copy(y_hbm, b_ref, sem, 0, r).wait()
        return 0

    lax.fori_loop(0, rows, start, 0)
    lax.fori_loop(0, rows, wait, 0)
    top = top_ref[...]
    a_hi, a_lo = _unpack_bf16_pair(a_ref[...])
    b_hi, b_lo = _unpack_bf16_pair(b_ref[...])
    moe_hi = a_hi * top[:, TOP_K:TOP_K + 1] + b_hi * top[:, TOP_K + 1:TOP_K + 2]
    moe_lo = a_lo * top[:, TOP_K:TOP_K + 1] + b_lo * top[:, TOP_K + 1:TOP_K + 2]
    pieces = []
    for j in range(moe_hi.shape[1] // group):
        pieces += [moe_hi[:, j * group:(j + 1) * group], moe_lo[:, j * group:(j + 1) * group]]
    x = x_ref[...] + jnp.concatenate(pieces, axis=1)
    ms = jnp.mean(x * x, axis=-1, keepdims=True)
    o_ref[...] = x * lax.rsqrt(ms + RMS_EPS) * g_ref[...]


def _combine_norm(x, y, dest, top, g):
    m, d = x.shape
    rows = min(COMBINE_ROWS, m)
    return pl.pallas_call(
        functools.partial(_combine_kernel, group=min(MOE_OUT_TN, d) // 2),
        grid_spec=pltpu.PrefetchScalarGridSpec(
            num_scalar_prefetch=1, grid=(m // rows,),
            in_specs=[pl.BlockSpec((rows, d), lambda i, dest: (i, 0)),
                      pl.BlockSpec(memory_space=pl.ANY),
                      pl.BlockSpec((rows, LANES), lambda i, dest: (i, 0)),
                      pl.BlockSpec((1, d), lambda i, dest: (0, 0))],
            out_specs=pl.BlockSpec((rows, d), lambda i, dest: (i, 0)),
            scratch_shapes=[pltpu.VMEM((rows, d // 2), jnp.uint32), pltpu.VMEM((rows, d // 2), jnp.uint32),
                            pltpu.SemaphoreType.DMA(())]),
        out_shape=jax.ShapeDtypeStruct((m, d), F32),
        compiler_params=_cparams(1), name="moe_combine_norm",
    )(dest, x, y, top, g.reshape(1, d))


def _dispatch_plan(top, n_experts, tm):
    m = top.shape[0]
    n_pairs = m * TOP_K
    expert = top[:, :TOP_K].astype(jnp.int32).reshape(n_pairs)
    onehot = (expert[:, None] == jnp.arange(n_experts, dtype=jnp.int32)[None, :]).astype(jnp.int32)
    before = jnp.cumsum(onehot, axis=0) - onehot
    rank = jnp.sum(before * onehot, axis=1)
    counts = jnp.sum(onehot, axis=0)
    tiles = (counts + tm - 1) // tm
    tile_end = jnp.cumsum(tiles)
    group_start = (tile_end - tiles) * tm
    dest = group_start[expert] + rank
    n_tiles = n_pairs // tm + n_experts
    n_rows = n_tiles * tm
    src_token = jnp.zeros((n_rows,), jnp.int32).at[dest].set(jnp.arange(n_pairs, dtype=jnp.int32) // TOP_K)
    tile_id = jnp.arange(n_tiles, dtype=jnp.int32)
    tile_expert = jnp.minimum(jnp.sum((tile_end[None, :] <= tile_id[:, None]).astype(jnp.int32), axis=1),
                              n_experts - 1)
    n_valid = tile_end[-1:].astype(jnp.int32)
    prev_expert = jnp.concatenate([jnp.full((1,), -1, jnp.int32), tile_expert[:-1]])
    first = (tile_id < n_valid[0]) & (tile_expert != prev_expert)
    later_first = first[None, :] & (tile_id[None, :] > tile_id[:, None])
    next_first = jnp.min(jnp.where(later_first, tile_id[None, :], n_tiles), axis=1)
    is_last_run = next_first == n_tiles
    next_expert = jnp.where(is_last_run, tile_expert[0], tile_expert[jnp.minimum(next_first, n_tiles - 1)])
    plan = jnp.stack([tile_expert, first.astype(jnp.int32), next_expert, is_last_run.astype(jnp.int32)])
    return dest.astype(jnp.int32), src_token, plan.astype(jnp.int32), n_valid


def kernel(x, attn_norm, ffn_norm, gla_w_in, gla_w_gate, gla_b_gate, gla_onorm, gla_w_out,
           sb_w_in, sb_w_out, dense_w_in, dense_w_out, moe_router, moe_w_in, moe_w_out,
           final_norm):
    batch, seq, d = x.shape
    m = batch * seq
    x = x.reshape(m, d)

    rank, dk = gla_w_gate.shape[1:]
    hv = gla_onorm.shape[1]
    dv = GLA_HEADS * hv
    n_proj = 2 * dk + 2 * dv
    h = _rmsnorm(x, attn_norm[0], BF16)
    w_in_t = jnp.transpose(gla_w_in[0])
    proj = _matmul_nt(h, w_in_t, n_cols=n_proj, name="gla_in")
    b, chunk_decay = _gla_gate(h, w_in_t, n_proj, gla_w_gate[0], gla_b_gate[0], min(GLA_CHUNK, seq))
    o = _gla_core(proj, b, chunk_decay, gla_onorm[0], batch, seq, GLA_HEADS, dk, dv)
    x = _matmul(o, gla_w_out[0], n_cols=d, tk=dv, res=x, name="gla_out")

    d_ff = dense_w_out.shape[1]
    h = _rmsnorm(x, ffn_norm[0], BF16)
    act = _swiglu_in(h, dense_w_in[0], d_ff, name="dense_in")
    half = d_ff // 2
    x = _matmul(act, dense_w_out[0], n_cols=d, tk=half, k_blk=0, tn=256, res=x, name="dense_out0")
    x = _matmul(act, dense_w_out[0], n_cols=d, tk=half, k_blk=1, tn=256, res=x, name="dense_out1")

    h = _rmsnorm(x, attn_norm[1], BF16)
    qkv = _matmul(h, sb_w_in[0], n_cols=3 * d, tk=d, name="sb_in")
    o = _sb_core(qkv, batch, seq, SB_HEADS, d)
    x = _matmul(o, sb_w_out[0], n_cols=d, tk=d, res=x, name="sb_out")

    n_experts = moe_router.shape[2]
    tm = min(MOE_TM, m)
    h_packed, top = _router(x, ffn_norm[1], moe_router[0])
    dest, src_token, plan, n_valid = _dispatch_plan(top, n_experts, tm)
    xs = _gather_rows(h_packed, src_token, n_valid * tm)
    act = _moe_in(xs, moe_w_in[0], plan, n_valid, tm)
    y = _moe_out(act, moe_w_out[0], plan, n_valid, tm)
    out = _combine_norm(x, y, dest, top, final_norm)
    return out.reshape(batch, seq, d)
```

```python
import functools

import jax
import jax.numpy as jnp
from jax import lax
from jax.experimental import pallas as pl
from jax.experimental.pallas import tpu as pltpu

F32 = jnp.float32
BF16 = jnp.bfloat16

RMS_EPS = 1e-6
LOG2_E = 1.4426950408889634
GLA_HEADS = 4
GLA_GATE_TAU = 16.0
GLA_CHUNK = 256
GLA_SUB = 16
GLA_MILD_DECAY = 60.0
SB_HEADS = 32
SB_TQ = 512
SB_TK = 256
TOP_K = 2
LANES = 128
SUBLANES = 8
VMEM_LIMIT_BYTES = 56 * 1024 * 1024

MM_TM = 1024
MM_TN = 512
MOE_TM = 512
MOE_IN_TN = 512
MOE_OUT_TN = 512
ROW_DMA_UNROLL = 8
GATHER_ROWS = 512
COMBINE_ROWS = 256


def _cparams(n_axes):
    return pltpu.CompilerParams(
        dimension_semantics=("arbitrary",) * n_axes,
        vmem_limit_bytes=VMEM_LIMIT_BYTES)


def _silu(x):
    return x / (1.0 + jnp.exp(-x))


def _log_sigmoid(x):
    return jnp.minimum(x, 0.0) - jnp.log1p(jnp.exp(-jnp.abs(x)))


def _rmsnorm_kernel(x_ref, g_ref, o_ref):
    x = x_ref[...]
    ms = jnp.mean(x * x, axis=-1, keepdims=True)
    o_ref[...] = (x * lax.rsqrt(ms + RMS_EPS) * g_ref[...]).astype(o_ref.dtype)


def _rmsnorm(x, g, out_dtype):
    m, d = x.shape
    tm = min(256, m)
    return pl.pallas_call(
        _rmsnorm_kernel,
        grid=(m // tm,),
        in_specs=[pl.BlockSpec((tm, d), lambda i: (i, 0)),
                  pl.BlockSpec((1, d), lambda i: (0, 0))],
        out_specs=pl.BlockSpec((tm, d), lambda i: (i, 0)),
        out_shape=jax.ShapeDtypeStruct((m, d), out_dtype),
        compiler_params=_cparams(1),
        name="rmsnorm",
    )(x, g.reshape(1, d))


def _mm_plain_kernel(x_ref, w_ref, o_ref):
    acc = jnp.dot(x_ref[...], w_ref[...].astype(BF16), preferred_element_type=F32)
    o_ref[...] = acc.astype(o_ref.dtype)


def _mm_nt_kernel(x_ref, wt_ref, o_ref):
    acc = lax.dot_general(x_ref[...], wt_ref[...].astype(BF16), (((1,), (1,)), ((), ())),
                          preferred_element_type=F32)
    o_ref[...] = acc.astype(o_ref.dtype)


def _matmul_nt(x, wt, *, n_cols, tm=None, tn=None, out_dtype=BF16, name="matmul_nt"):
    m, k = x.shape
    tm = min(tm or MM_TM, m)
    tn = min(tn or MM_TN, n_cols)
    assert m % tm == 0 and n_cols % tn == 0
    return pl.pallas_call(
        _mm_nt_kernel, grid=(m // tm, n_cols // tn),
        in_specs=[pl.BlockSpec((tm, k), lambda i, j: (i, 0)),
                  pl.BlockSpec((tn, k), lambda i, j: (j, 0))],
        out_specs=pl.BlockSpec((tm, tn), lambda i, j: (i, j)),
        out_shape=jax.ShapeDtypeStruct((m, n_cols), out_dtype),
        compiler_params=_cparams(2), name=name,
    )(x, wt)


def _mm_res_kernel(x_ref, w_ref, res_ref, o_ref):
    acc = jnp.dot(x_ref[...], w_ref[...].astype(BF16), preferred_element_type=F32)
    o_ref[...] = res_ref[...] + acc


def _mm_swiglu_kernel(x_ref, wg_ref, wu_ref, o_ref):
    x = x_ref[...]
    g = jnp.dot(x, wg_ref[...].astype(BF16), preferred_element_type=F32)
    u = jnp.dot(x, wu_ref[...].astype(BF16), preferred_element_type=F32)
    o_ref[...] = (_silu(g) * u).astype(o_ref.dtype)


def _matmul(x, w, *, n_cols, tk, k_blk=0, w_col_blk=0, tm=None, tn=None,
            out_dtype=BF16, res=None, name="matmul"):
    m = x.shape[0]
    tm = min(tm or MM_TM, m)
    tn = min(tn or MM_TN, n_cols)
    assert m % tm == 0 and n_cols % tn == 0
    grid = (m // tm, n_cols // tn)
    x_spec = pl.BlockSpec((tm, tk), lambda i, j: (i, k_blk))
    w_spec = pl.BlockSpec((tk, tn), lambda i, j: (k_blk, w_col_blk + j))
    o_spec = pl.BlockSpec((tm, tn), lambda i, j: (i, j))
    if res is None:
        kern, in_specs, args = _mm_plain_kernel, [x_spec, w_spec], (x, w)
    else:
        kern, in_specs, args = _mm_res_kernel, [x_spec, w_spec, o_spec], (x, w, res)
        out_dtype = F32
    return pl.pallas_call(
        kern, grid=grid, in_specs=in_specs, out_specs=o_spec,
        out_shape=jax.ShapeDtypeStruct((m, n_cols), out_dtype),
        compiler_params=_cparams(2), name=name,
    )(*args)


def _swiglu_in(x, w, d_ff, *, tm=None, tn=None, name="swiglu_in"):
    m, k = x.shape
    tm = min(tm or MM_TM, m)
    tn = tn or 256
    assert m % tm == 0 and d_ff % tn == 0
    nb = d_ff // tn
    return pl.pallas_call(
        _mm_swiglu_kernel,
        grid=(m // tm, nb),
        in_specs=[pl.BlockSpec((tm, k), lambda i, j: (i, 0)),
                  pl.BlockSpec((k, tn), lambda i, j: (0, j)),
                  pl.BlockSpec((k, tn), lambda i, j: (0, nb + j))],
        out_specs=pl.BlockSpec((tm, tn), lambda i, j: (i, j)),
        out_shape=jax.ShapeDtypeStruct((m, d_ff), BF16),
        compiler_params=_cparams(2), name=name,
    )(x, w, w)


def _dot_f32(a, b):
    a_hi, a_lo = _split_bf16(a, 2)
    b_hi, b_lo = _split_bf16(b, 2)
    return (jnp.dot(a_hi, b_hi, preferred_element_type=F32) + jnp.dot(a_hi, b_lo, preferred_element_type=F32)
            + jnp.dot(a_lo, b_hi, preferred_element_type=F32))


def _split_bf16(x, terms):
    out = []
    for _ in range(terms):
        t = x.astype(BF16)
        out.append(t)
        x = x - t.astype(F32)
    return out


def _dot_exact_lhs(lhs, x):
    return sum(jnp.dot(lhs, t, preferred_element_type=F32) for t in _split_bf16(x, 3))


def _gla_gate_kernel(h_ref, wa_ref, wg_ref, bg_ref, o_ref, span_ref, *, chunk, rank):
    tm, d = h_ref.shape
    w_a = jnp.concatenate([wa_ref[...].astype(BF16), jnp.zeros((LANES - rank, d), BF16)], axis=0)
    a_low = lax.dot_general(h_ref[...], w_a, (((1,), (1,)), ((), ())),
                            preferred_element_type=F32)
    xg = _dot_f32(a_low, wg_ref[...]) + bg_ref[...]
    log_alpha = _log_sigmoid(xg) * (1.0 / GLA_GATE_TAU)
    row = lax.broadcasted_iota(jnp.int32, (tm, tm), 0)
    col = lax.broadcasted_iota(jnp.int32, (tm, tm), 1)
    same_chunk = (row // chunk) == (col // chunk)
    tril = jnp.where(same_chunk & (col <= row), 1.0, 0.0).astype(BF16)
    b = _dot_exact_lhs(tril, log_alpha)
    o_ref[...] = b
    spans = [jnp.max(b[c * chunk:c * chunk + 1, :] - b[(c + 1) * chunk - 1:(c + 1) * chunk, :],
                     axis=-1, keepdims=True) for c in range(tm // chunk)]
    spans = jnp.concatenate(spans + [jnp.zeros((span_ref.shape[0] - len(spans), 1), F32)], axis=0)
    span_ref[...] = jnp.broadcast_to(spans, span_ref.shape)


def _gla_gate(h, w_in_t, n_proj, w_gate, b_gate, chunk):
    m, d = h.shape
    rank, dk = w_gate.shape
    assert n_proj % rank == 0 and rank % 16 == 0 and rank <= LANES
    tm = min(256, m)
    per_step = tm // chunk
    assert tm % chunk == 0 and per_step <= SUBLANES
    wg_pad = jnp.zeros((LANES, dk), F32).at[:rank, :].set(w_gate)
    b, spans = pl.pallas_call(
        functools.partial(_gla_gate_kernel, chunk=chunk, rank=rank),
        grid=(m // tm,),
        in_specs=[pl.BlockSpec((tm, d), lambda i: (i, 0)),
                  pl.BlockSpec((rank, d), lambda i: (n_proj // rank, 0)),
                  pl.BlockSpec((LANES, dk), lambda i: (0, 0)),
                  pl.BlockSpec((1, dk), lambda i: (0, 0))],
        out_specs=[pl.BlockSpec((tm, dk), lambda i: (i, 0)),
                   pl.BlockSpec((None, SUBLANES, LANES), lambda i: (i, 0, 0))],
        out_shape=[jax.ShapeDtypeStruct((m, dk), F32),
                   jax.ShapeDtypeStruct((m // tm, SUBLANES, LANES), F32)],
        compiler_params=_cparams(1), name="gla_gate",
    )(h, w_in_t, wg_pad, b_gate.reshape(1, dk))
    return b, spans[:, :per_step, 0].reshape(m // chunk)


def _gla_core_kernel(mild_ref, q_ref, k_ref, v_ref, r_ref, b_ref, gain_ref, o_ref, s_ref, sc_ref, *, scale, sub):
    chunk, hk = q_ref.shape
    mild_decay = mild_ref[pl.program_id(0) * pl.num_programs(2) + pl.program_id(2)] == 1

    @pl.when(pl.program_id(2) == 0)
    def _():
        s_ref[...] = jnp.zeros_like(s_ref)

    b = b_ref[...]
    q = q_ref[...].astype(F32) * scale
    k = k_ref[...].astype(F32)
    v = v_ref[...]
    state = s_ref[...]

    o = jnp.dot((q * jnp.exp(b)).astype(BF16), state.astype(BF16), preferred_element_type=F32)

    @pl.when(mild_decay)
    def _():
        b0 = b[0:1, :]
        q_t = (q * jnp.exp(b - b0)).astype(BF16)
        k_t = (k * jnp.exp(b0 - b)).astype(BF16)
        full = lax.dot_general(q_t, k_t, (((1,), (1,)), ((), ())), preferred_element_type=F32)
        t_id = lax.broadcasted_iota(jnp.int32, (chunk, chunk), 0)
        s_id = lax.broadcasted_iota(jnp.int32, (chunk, chunk), 1)
        sc_ref[...] = jnp.where(s_id <= t_id, full, 0.0)

    @pl.when(jnp.logical_not(mild_decay))
    def _():
        row_id = lax.broadcasted_iota(jnp.int32, (sub, chunk), 0)
        key_id = lax.broadcasted_iota(jnp.int32, (sub, chunk), 1)
        for i in range(chunk // sub):
            lo = i * sub
            b_i = b[lo:lo + sub, :]
            q_i = q[lo:lo + sub, :]
            diag = jnp.zeros((sub, chunk), F32)
            for j in range(sub):
                s = lo + j
                decay = jnp.exp(jnp.minimum(b_i - b[s:s + 1, :], 0.0))
                col = jnp.sum(q_i * decay * k[s:s + 1, :], axis=-1, keepdims=True)
                diag = jnp.where(key_id == s, col, diag)
            scores = jnp.where(key_id - lo <= row_id, diag, 0.0)
            if i > 0:
                b_first = b[lo:lo + 1, :]
                q_t = (q_i * jnp.exp(b_i - b_first)).astype(BF16)
                k_t = (k * jnp.exp(jnp.minimum(b_first - b, 0.0))).astype(BF16)
                below = lax.dot_general(q_t, k_t, (((1,), (1,)), ((), ())), preferred_element_type=F32)
                scores = jnp.where(key_id < lo, below, scores)
            sc_ref[lo:lo + sub, :] = scores

    o = o + jnp.dot(sc_ref[...].astype(BF16), v, preferred_element_type=F32)

    b_last = b[chunk - 1:chunk, :]
    k_state = (k * jnp.exp(b_last - b)).astype(BF16)
    update = lax.dot_general(k_state, v, (((0,), (0,)), ((), ())), preferred_element_type=F32)
    decay_rows = jnp.broadcast_to(jnp.exp(b_last), (LANES, hk))
    decay_col = jnp.transpose(decay_rows)[:, 0:1]
    s_ref[...] = state * decay_col + update

    ms = jnp.mean(o * o, axis=-1, keepdims=True)
    o = o * lax.rsqrt(ms + RMS_EPS) * gain_ref[...]
    o_ref[...] = (o * _silu(r_ref[...].astype(F32))).astype(o_ref.dtype)


def _gla_core(proj, b, chunk_decay, gain, batch, seq, heads, dk, dv):
    m = batch * seq
    hk, hv = dk // heads, dv // heads
    chunk = min(GLA_CHUNK, seq)
    nc = seq // chunk
    k_blk0 = dk // hk
    v_blk0 = (2 * dk) // hv
    r_blk0 = (2 * dk + dv) // hv
    mild = (chunk_decay < GLA_MILD_DECAY).astype(jnp.int32)

    def rows(bi, ci):
        return bi * nc + ci

    return pl.pallas_call(
        functools.partial(_gla_core_kernel, scale=float(hk) ** -0.5, sub=min(GLA_SUB, chunk)),
        grid_spec=pltpu.PrefetchScalarGridSpec(
            num_scalar_prefetch=1, grid=(batch, heads, nc),
            in_specs=[pl.BlockSpec((chunk, hk), lambda bi, h, c, mild: (rows(bi, c), h)),
                      pl.BlockSpec((chunk, hk), lambda bi, h, c, mild: (rows(bi, c), k_blk0 + h)),
                      pl.BlockSpec((chunk, hv), lambda bi, h, c, mild: (rows(bi, c), v_blk0 + h)),
                      pl.BlockSpec((chunk, hv), lambda bi, h, c, mild: (rows(bi, c), r_blk0 + h)),
                      pl.BlockSpec((chunk, hk), lambda bi, h, c, mild: (rows(bi, c), h)),
                      pl.BlockSpec((1, hv), lambda bi, h, c, mild: (0, 0))],
            out_specs=pl.BlockSpec((chunk, hv), lambda bi, h, c, mild: (rows(bi, c), h)),
            scratch_shapes=[pltpu.VMEM((hk, hv), F32), pltpu.VMEM((chunk, chunk), F32)]),
        out_shape=jax.ShapeDtypeStruct((m, dv), BF16),
        compiler_params=_cparams(3), name="gla_core",
    )(mild, proj, proj, proj, proj, b, gain.reshape(1, hv))


def _sb_kernel(q_ref, k_ref, v_ref, o_ref, acc_ref, carry_ref, hl0_ref, hl1_ref, lsp0_ref, lsp1_ref,
               sl0_ref, sl1_ref, tot0_ref, tot1_ref, *, scale, tk):
    tq, dh = q_ref.shape
    ratio = tq // tk
    assert ratio % 2 == 0
    hl_refs, lsp_refs = (hl0_ref, hl1_ref), (lsp0_ref, lsp1_ref)
    sl_refs, tot_refs = (sl0_ref, sl1_ref), (tot0_ref, tot1_ref)
    qi = pl.program_id(2)
    q = (q_ref[...].astype(F32) * (scale * LOG2_E)).astype(BF16)

    later = lax.broadcasted_iota(jnp.int32, (tk, tk), 0)
    key = lax.broadcasted_iota(jnp.int32, (tk, tk), 1)
    suffix_ones = jnp.where((later > key) | (key == tk - 1), 1.0, 0.0).astype(BF16)

    acc_ref[...] = jnp.zeros_like(acc_ref)
    carry_ref[...] = jnp.zeros_like(carry_ref)

    def logit_stage(k_start, parity, strict):
        k_blk = k_ref[pl.ds(k_start, tk), :]
        z = lax.dot_general(q, k_blk, (((1,), (1,)), ((), ())), preferred_element_type=F32)
        neg_abs = pltpu.bitcast(pltpu.bitcast(z, jnp.uint32) | jnp.uint32(0x80000000), F32)
        ls_pos = jnp.minimum(z, 0.0) - jnp.log(1.0 + jnp.exp2(neg_abs)) * LOG2_E
        log_1m = ls_pos - z
        if strict is not None:
            log_1m = jnp.where(strict, log_1m, 0.0)
        hl_refs[parity][...] = log_1m.astype(BF16)
        lsp_refs[parity][...] = ls_pos

    def suffix_stage(parity):
        sums = jnp.dot(hl_refs[parity][...], suffix_ones, preferred_element_type=F32)
        sl_refs[parity][...] = lsp_refs[parity][...] + jnp.where(lane == tk - 1, 0.0, sums)
        tot_refs[parity][...] = jnp.broadcast_to(sums[:, tk - 1:tk], (tq, LANES))

    def value_stage(k_start, parity, strict):
        v_blk = v_ref[pl.ds(k_start, tk), :]
        carry = carry_ref[...]
        w = jnp.exp2(sl_refs[parity][...] + jnp.concatenate([carry] * (tk // LANES), axis=1))
        if strict is not None:
            w = jnp.where(strict, w, 0.0)
        acc_ref[...] += jnp.dot(w.astype(BF16), v_blk, preferred_element_type=F32)
        carry_ref[...] = carry + tot_refs[parity][...]

    def k_start_of(t):
        return pl.multiple_of(qi * tq + (ratio - 1 - t) * tk, tk)

    row = lax.broadcasted_iota(jnp.int32, (tq, tk), 0)
    lane = lax.broadcasted_iota(jnp.int32, (tq, tk), 1)

    def strict_mask(t):
        return (lane + (ratio - 1 - t) * tk) < row if t < ratio else None

    def run_step(s, parity, mask_s, mask_ahead, do_logit=True, do_suffix=True):
        if do_logit:
            logit_stage(k_start_of(s + 2), parity, mask_ahead)
        if do_suffix:
            suffix_stage(1 - parity)
        value_stage(k_start_of(s), parity, mask_s)

    def prologue():
        logit_stage(k_start_of(0), 0, strict_mask(0))
        logit_stage(k_start_of(1), 1, strict_mask(1))
        suffix_stage(0)

    @pl.when(qi == 0)
    def _():
        prologue()
        for s in range(ratio):
            run_step(s, s % 2, strict_mask(s), strict_mask(s + 2),
                     do_logit=s + 2 < ratio, do_suffix=s + 1 < ratio)

    @pl.when(qi > 0)
    def _():
        n_tiles = ratio * (qi + 1)
        prologue()
        for s in range(ratio):
            run_step(s, s % 2, strict_mask(s), strict_mask(s + 2))

        def body(it, _):
            for j in range(2):
                run_step(ratio + 2 * it + j, j, None, None)
            return 0

        lax.fori_loop(0, (ratio * qi - 2) // 2, body, 0)
        run_step(n_tiles - 2, 0, None, None, do_logit=False)
        run_step(n_tiles - 1, 1, None, None, do_logit=False, do_suffix=False)

    o_ref[...] = acc_ref[...].astype(o_ref.dtype)


def _sb_core(qkv, batch, seq, heads, d_model):
    dh = d_model // heads
    tq = min(SB_TQ, seq)
    tk = min(SB_TK, tq)
    nq = seq // tq
    return pl.pallas_call(
        functools.partial(_sb_kernel, scale=float(dh) ** -0.5, tk=tk),
        grid=(batch, heads, nq),
        in_specs=[pl.BlockSpec((tq, dh), lambda b, h, i: (b * nq + i, h)),
                  pl.BlockSpec((seq, dh), lambda b, h, i: (b, heads + h)),
                  pl.BlockSpec((seq, dh), lambda b, h, i: (b, 2 * heads + h))],
        out_specs=pl.BlockSpec((tq, dh), lambda b, h, i: (b * nq + i, h)),
        out_shape=jax.ShapeDtypeStruct((batch * seq, d_model), BF16),
        scratch_shapes=[pltpu.VMEM((tq, dh), F32), pltpu.VMEM((tq, LANES), F32),
                        pltpu.VMEM((tq, tk), BF16), pltpu.VMEM((tq, tk), BF16),
                        pltpu.VMEM((tq, tk), F32), pltpu.VMEM((tq, tk), F32),
                        pltpu.VMEM((tq, tk), F32), pltpu.VMEM((tq, tk), F32),
                        pltpu.VMEM((tq, LANES), F32), pltpu.VMEM((tq, LANES), F32)],
        compiler_params=_cparams(3), name="sb_core",
    )(qkv, qkv, qkv)


def _pack_bf16_pair(hi, lo):
    hi_bits = pltpu.bitcast(hi.astype(BF16).astype(F32), jnp.uint32)
    lo_bits = pltpu.bitcast(lo.astype(BF16).astype(F32), jnp.uint32)
    return hi_bits | (lo_bits >> jnp.uint32(16))


def _unpack_bf16_pair(words):
    hi = pltpu.bitcast(words & jnp.uint32(0xFFFF0000), F32)
    lo = pltpu.bitcast(words << jnp.uint32(16), F32)
    return hi, lo


def _router_kernel(x_ref, g_ref, wr_ref, h_ref, top_ref, *, n_experts):
    x = x_ref[...]
    half = x.shape[1] // 2
    ms = jnp.mean(x * x, axis=-1, keepdims=True)
    h = x * lax.rsqrt(ms + RMS_EPS) * g_ref[...]
    h_ref[...] = _pack_bf16_pair(h[:, :half], h[:, half:])
    logits = _dot_f32(h, wr_ref[...])
    lane = lax.broadcasted_iota(jnp.int32, logits.shape, 1)
    logits = jnp.where(lane < n_experts, logits, -jnp.inf)
    v1 = jnp.max(logits, axis=-1, keepdims=True)
    i1 = jnp.min(jnp.where(logits == v1, lane, LANES), axis=-1, keepdims=True)
    rest = jnp.where(lane == i1, -jnp.inf, logits)
    v2 = jnp.max(rest, axis=-1, keepdims=True)
    i2 = jnp.min(jnp.where(rest == v2, lane, LANES), axis=-1, keepdims=True)
    e = jnp.exp(v2 - v1)
    g1 = 1.0 / (1.0 + e)
    g2 = e * g1
    out = jnp.where(lane == 0, i1.astype(F32),
                    jnp.where(lane == 1, i2.astype(F32),
                              jnp.where(lane == 2, g1, jnp.where(lane == 3, g2, 0.0))))
    top_ref[...] = out


def _router(x, g, w_router):
    m, d = x.shape
    n_experts = w_router.shape[1]
    tm = min(256, m)
    wr_pad = jnp.zeros((d, LANES), F32).at[:, :n_experts].set(w_router)
    return pl.pallas_call(
        functools.partial(_router_kernel, n_experts=n_experts),
        grid=(m // tm,),
        in_specs=[pl.BlockSpec((tm, d), lambda i: (i, 0)),
                  pl.BlockSpec((1, d), lambda i: (0, 0)),
                  pl.BlockSpec((d, LANES), lambda i: (0, 0))],
        out_specs=[pl.BlockSpec((tm, d // 2), lambda i: (i, 0)),
                   pl.BlockSpec((tm, LANES), lambda i: (i, 0))],
        out_shape=[jax.ShapeDtypeStruct((m, d // 2), jnp.uint32),
                   jax.ShapeDtypeStruct((m, LANES), F32)],
        compiler_params=_cparams(1), name="moe_router",
    )(x, g.reshape(1, d), wr_pad)


def _row_copy(src_hbm, dst_ref, sem, src_row, dst_row):
    return pltpu.make_async_copy(src_hbm.at[pl.ds(src_row, 1), :],
                                 dst_ref.at[pl.ds(dst_row, 1), :], sem)


def _gather_kernel(idx_ref, nrows_ref, src_hbm, o_ref, buf_ref, sem):
    rows = o_ref.shape[0]
    base = pl.program_id(0) * rows

    @pl.when(base < nrows_ref[0])
    def _():
        def start(r, _):
            _row_copy(src_hbm, buf_ref, sem, idx_ref[base + r], r).start()
            return 0

        def wait(r, _):
            _row_copy(src_hbm, buf_ref, sem, 0, r).wait()
            return 0

        lax.fori_loop(0, rows, start, 0, unroll=ROW_DMA_UNROLL)
        lax.fori_loop(0, rows, wait, 0, unroll=ROW_DMA_UNROLL)
        half = buf_ref.shape[1]
        hi, lo = _unpack_bf16_pair(buf_ref[...])
        o_ref[:, :half] = hi.astype(o_ref.dtype)
        o_ref[:, half:] = lo.astype(o_ref.dtype)

    @pl.when(base >= nrows_ref[0])
    def _():
        o_ref[...] = jnp.zeros_like(o_ref)


def _gather_rows(src_packed, idx, n_rows_valid):
    n = idx.shape[0]
    half = src_packed.shape[1]
    rows = min(GATHER_ROWS, n)
    assert n % rows == 0
    return pl.pallas_call(
        _gather_kernel,
        grid_spec=pltpu.PrefetchScalarGridSpec(
            num_scalar_prefetch=2, grid=(n // rows,),
            in_specs=[pl.BlockSpec(memory_space=pl.ANY)],
            out_specs=pl.BlockSpec((rows, 2 * half), lambda i, idx, nr: (i, 0)),
            scratch_shapes=[pltpu.VMEM((rows, half), jnp.uint32), pltpu.SemaphoreType.DMA(())]),
        out_shape=jax.ShapeDtypeStruct((n, 2 * half), BF16),
        compiler_params=_cparams(1), name="moe_gather",
    )(idx, n_rows_valid, src_packed)


def _weight_stream(plan_ref, copies, refill):
    j, i = pl.program_id(0), pl.program_id(1)

    @pl.when((j == 0) & (i == 0))
    def _():
        for cp in copies(plan_ref[0, 0], 0):
            cp.start()

    @pl.when(plan_ref[1, i] == 1)
    def _():
        for cp in copies(plan_ref[0, i], j):
            cp.wait()
        refill()
        j_next = j + plan_ref[3, i]

        @pl.when(j_next < pl.num_programs(0))
        def _():
            for cp in copies(plan_ref[2, i], j_next):
                cp.start()


def _moe_in_kernel(plan_ref, nvalid_ref, x_ref, w_hbm, o_ref, wf_ref, wb_ref, sem, *, d_ff, last_shift):
    j, i = pl.program_id(0), pl.program_id(1)
    tn = o_ref.shape[1]

    def copies(expert, jb):
        c = pl.multiple_of(jnp.minimum(jb * (tn // LANES), (d_ff - tn) // LANES) * LANES, LANES)
        return (pltpu.make_async_copy(w_hbm.at[expert, :, pl.ds(c, tn)], wf_ref.at[:, pl.ds(0, tn)], sem.at[0]),
                pltpu.make_async_copy(w_hbm.at[expert, :, pl.ds(pl.multiple_of(d_ff + c, LANES), tn)],
                                      wf_ref.at[:, pl.ds(tn, tn)], sem.at[1]))

    def refill():
        wb_ref[...] = wf_ref[...].astype(BF16)

    _weight_stream(plan_ref, copies, refill)

    @pl.when(i < nvalid_ref[0])
    def _():
        gu = jnp.dot(x_ref[...], wb_ref[...], preferred_element_type=F32)
        act = (_silu(gu[:, :tn]) * gu[:, tn:]).astype(o_ref.dtype)
        if last_shift == 0:
            o_ref[...] = act
        else:
            is_last = j == pl.num_programs(0) - 1

            @pl.when(is_last)
            def _():
                o_ref[:, :tn - last_shift] = act[:, last_shift:]
                o_ref[:, tn - last_shift:] = jnp.zeros((act.shape[0], last_shift), o_ref.dtype)

            @pl.when(jnp.logical_not(is_last))
            def _():
                o_ref[...] = act

    @pl.when(i >= nvalid_ref[0])
    def _():
        o_ref[...] = jnp.zeros_like(o_ref)


def _moe_in(xs, w_in, plan, n_valid, tm):
    r, d = xs.shape
    d_ff = w_in.shape[2] // 2
    tn = min(MOE_IN_TN, d_ff)
    nb = pl.cdiv(d_ff, tn)
    assert d_ff % LANES == 0 and tn % LANES == 0

    def x_map(j, i, plan, nv):
        return (jnp.minimum(i, nv[0] - 1), 0)

    return pl.pallas_call(
        functools.partial(_moe_in_kernel, d_ff=d_ff, last_shift=nb * tn - d_ff),
        grid_spec=pltpu.PrefetchScalarGridSpec(
            num_scalar_prefetch=2, grid=(nb, r // tm),
            in_specs=[pl.BlockSpec((tm, d), x_map),
                      pl.BlockSpec(memory_space=pl.ANY)],
            out_specs=pl.BlockSpec((tm, tn), lambda j, i, plan, nv: (i, j)),
            scratch_shapes=[pltpu.VMEM((d, 2 * tn), F32), pltpu.VMEM((d, 2 * tn), BF16),
                            pltpu.SemaphoreType.DMA((2,))]),
        out_shape=jax.ShapeDtypeStruct((r, nb * tn), BF16),
        compiler_params=_cparams(2), name="moe_in",
    )(plan, n_valid, xs, w_in)


def _moe_out_kernel(plan_ref, nvalid_ref, x_ref, w_hbm, o_ref, wf_ref, wb_ref, sem):
    i = pl.program_id(1)
    tn = wf_ref.shape[1]

    def copies(expert, jb):
        c = pl.multiple_of(jb * tn, LANES)
        return (pltpu.make_async_copy(w_hbm.at[expert, :, pl.ds(c, tn)], wf_ref, sem.at[0]),)

    def refill():
        wb_ref[...] = wf_ref[...].astype(BF16)

    _weight_stream(plan_ref, copies, refill)

    @pl.when(i < nvalid_ref[0])
    def _():
        y = jnp.dot(x_ref[...], wb_ref[...], preferred_element_type=F32)
        o_ref[...] = _pack_bf16_pair(y[:, :tn // 2], y[:, tn // 2:])

    @pl.when(i >= nvalid_ref[0])
    def _():
        o_ref[...] = jnp.zeros_like(o_ref)


def _moe_out(act, w_out, plan, n_valid, tm):
    r = act.shape[0]
    d_ff, d = w_out.shape[1:]
    tn = min(MOE_OUT_TN, d)
    assert d % tn == 0 and tn % (2 * LANES) == 0

    def x_map(j, i, plan, nv):
        return (jnp.minimum(i, nv[0] - 1), 0)

    return pl.pallas_call(
        _moe_out_kernel,
        grid_spec=pltpu.PrefetchScalarGridSpec(
            num_scalar_prefetch=2, grid=(d // tn, r // tm),
            in_specs=[pl.BlockSpec((tm, d_ff), x_map),
                      pl.BlockSpec(memory_space=pl.ANY)],
            out_specs=pl.BlockSpec((tm, tn // 2), lambda j, i, plan, nv: (i, j)),
            scratch_shapes=[pltpu.VMEM((d_ff, tn), F32), pltpu.VMEM((d_ff, tn), BF16),
                            pltpu.SemaphoreType.DMA((1,))]),
        out_shape=jax.ShapeDtypeStruct((r, d // 2), jnp.uint32),
        compiler_params=_cparams(2), name="moe_out",
    )(plan, n_valid, act, w_out)


def _combine_kernel(dest_ref, x_ref, y_hbm, top_ref, g_ref, o_ref, a_ref, b_ref, sem, *, group):
    rows = x_ref.shape[0]
    base = pl.program_id(0) * rows

    def start(r, _):
        _row_copy(y_hbm, a_ref, sem, dest_ref[2 * (base + r)], r).start()
        _row_copy(y_hbm, b_ref, sem, dest_ref[2 * (base + r) + 1], r).start()
        return 0

    def wait(r, _):
        _row_copy(y_hbm, a_ref, sem, 0, r).wait()
        _row_copy(y_hbm, b_ref, sem, 0, r).wait()
        return 0

    lax.fori_loop(0, rows, start, 0, unroll=ROW_DMA_UNROLL)
    lax.fori_loop(0, rows, wait, 0, unroll=ROW_DMA_UNROLL)
    top = top_ref[...]
    a_hi, a_lo = _unpack_bf16_pair(a_ref[...])
    b_hi, b_lo = _unpack_bf16_pair(b_ref[...])
    moe_hi = a_hi * top[:, TOP_K:TOP_K + 1] + b_hi * top[:, TOP_K + 1:TOP_K + 2]
    moe_lo = a_lo * top[:, TOP_K:TOP_K + 1] + b_lo * top[:, TOP_K + 1:TOP_K + 2]
    pieces = []
    for j in range(moe_hi.shape[1] // group):
        pieces += [moe_hi[:, j * group:(j + 1) * group], moe_lo[:, j * group:(j + 1) * group]]
    x = x_ref[...] + jnp.concatenate(pieces, axis=1)
    ms = jnp.mean(x * x, axis=-1, keepdims=True)
    o_ref[...] = x * lax.rsqrt(ms + RMS_EPS) * g_ref[...]


def _combine_norm(x, y, dest, top, g):
    m, d = x.shape
    rows = min(COMBINE_ROWS, m)
    return pl.pallas_call(
        functools.partial(_combine_kernel, group=min(MOE_OUT_TN, d) // 2),
        grid_spec=pltpu.PrefetchScalarGridSpec(
            num_scalar_prefetch=1, grid=(m // rows,),
            in_specs=[pl.BlockSpec((rows, d), lambda i, dest: (i, 0)),
                      pl.BlockSpec(memory_space=pl.ANY),
                      pl.BlockSpec((rows, LANES), lambda i, dest: (i, 0)),
                      pl.BlockSpec((1, d), lambda i, dest: (0, 0))],
            out_specs=pl.BlockSpec((rows, d), lambda i, dest: (i, 0)),
            scratch_shapes=[pltpu.VMEM((rows, d // 2), jnp.uint32), pltpu.VMEM((rows, d // 2), jnp.uint32),
                            pltpu.SemaphoreType.DMA(())]),
        out_shape=jax.ShapeDtypeStruct((m, d), F32),
        compiler_params=_cparams(1), name="moe_combine_norm",
    )(dest, x, y, top, g.reshape(1, d))


def _dispatch_plan(top, n_experts, tm):
    m = top.shape[0]
    n_pairs = m * TOP_K
    expert = top[:, :TOP_K].astype(jnp.int32).reshape(n_pairs)
    onehot = (expert[:, None] == jnp.arange(n_experts, dtype=jnp.int32)[None, :]).astype(jnp.int32)
    before = jnp.cumsum(onehot, axis=0) - onehot
    rank = jnp.sum(before * onehot, axis=1)
    counts = jnp.sum(onehot, axis=0)
    tiles = (counts + tm - 1) // tm
    tile_end = jnp.cumsum(tiles)
    group_start = (tile_end - tiles) * tm
    dest = group_start[expert] + rank
    n_tiles = n_pairs // tm + n_experts
    n_rows = n_tiles * tm
    src_token = jnp.zeros((n_rows,), jnp.int32).at[dest].set(jnp.arange(n_pairs, dtype=jnp.int32) // TOP_K)
    tile_id = jnp.arange(n_tiles, dtype=jnp.int32)
    tile_expert = jnp.minimum(jnp.sum((tile_end[None, :] <= tile_id[:, None]).astype(jnp.int32), axis=1),
                              n_experts - 1)
    n_valid = tile_end[-1:].astype(jnp.int32)
    prev_expert = jnp.concatenate([jnp.full((1,), -1, jnp.int32), tile_expert[:-1]])
    first = (tile_id < n_valid[0]) & (tile_expert != prev_expert)
    later_first = first[None, :] & (tile_id[None, :] > tile_id[:, None])
    next_first = jnp.min(jnp.where(later_first, tile_id[None, :], n_tiles), axis=1)
    is_last_run = next_first == n_tiles
    next_expert = jnp.where(is_last_run, tile_expert[0], tile_expert[jnp.minimum(next_first, n_tiles - 1)])
    plan = jnp.stack([tile_expert, first.astype(jnp.int32), next_expert, is_last_run.astype(jnp.int32)])
    return dest.astype(jnp.int32), src_token, plan.astype(jnp.int32), n_valid


def kernel(x, attn_norm, ffn_norm, gla_w_in, gla_w_gate, gla_b_gate, gla_onorm, gla_w_out,
           sb_w_in, sb_w_out, dense_w_in, dense_w_out, moe_router, moe_w_in, moe_w_out,
           final_norm):
    batch, seq, d = x.shape
    m = batch * seq
    x = x.reshape(m, d)

    rank, dk = gla_w_gate.shape[1:]
    hv = gla_onorm.shape[1]
    dv = GLA_HEADS * hv
    n_proj = 2 * dk + 2 * dv
    h = _rmsnorm(x, attn_norm[0], BF16)
    w_in_t = jnp.transpose(gla_w_in[0])
    proj = _matmul_nt(h, w_in_t, n_cols=n_proj, name="gla_in")
    b, chunk_decay = _gla_gate(h, w_in_t, n_proj, gla_w_gate[0], gla_b_gate[0], min(GLA_CHUNK, seq))
    o = _gla_core(proj, b, chunk_decay, gla_onorm[0], batch, seq, GLA_HEADS, dk, dv)
    x = _matmul(o, gla_w_out[0], n_cols=d, tk=dv, res=x, name="gla_out")

    d_ff = dense_w_out.shape[1]
    h = _rmsnorm(x, ffn_norm[0], BF16)
    act = _swiglu_in(h, dense_w_in[0], d_ff, name="dense_in")
    half = d_ff // 2
    x = _matmul(act, dense_w_out[0], n_cols=d, tk=half, k_blk=0, tn=256, res=x, name="dense_out0")
    x = _matmul(act, dense_w_out[0], n_cols=d, tk=half, k_blk=1, tn=256, res=x, name="dense_out1")

    h = _rmsnorm(x, attn_norm[1], BF16)
    qkv = _matmul(h, sb_w_in[0], n_cols=3 * d, tk=d, name="sb_in")
    o = _sb_core(qkv, batch, seq, SB_HEADS, d)
    x = _matmul(o, sb_w_out[0], n_cols=d, tk=d, res=x, name="sb_out")

    n_experts = moe_router.shape[2]
    tm = min(MOE_TM, m)
    h_packed, top = _router(x, ffn_norm[1], moe_router[0])
    dest, src_token, plan, n_valid = _dispatch_plan(top, n_experts, tm)
    xs = _gather_rows(h_packed, src_token, n_valid * tm)
    act = _moe_in(xs, moe_w_in[0], plan, n_valid, tm)
    y = _moe_out(act, moe_w_out[0], plan, n_valid, tm)
    out = _combine_norm(x, y, dest, top, final_norm)
    return out.reshape(batch, seq, d)
```

```python
import functools

import jax
import jax.numpy as jnp
from jax import lax
from jax.experimental import pallas as pl
from jax.experimental.pallas import tpu as pltpu

F32 = jnp.float32
BF16 = jnp.bfloat16

RMS_EPS = 1e-6
LOG2_E = 1.4426950408889634
GLA_HEADS = 4
GLA_GATE_TAU = 16.0
GLA_CHUNK = 256
GLA_SUB = 16
GLA_MILD_DECAY = 60.0
SB_HEADS = 32
SB_TQ = 512
SB_TK = 256
TOP_K = 2
LANES = 128
SUBLANES = 8
VMEM_LIMIT_BYTES = 56 * 1024 * 1024

MM_TM = 1024
MM_TN = 512
MOE_TM = 512
MOE_IN_TN = 512
MOE_OUT_TN = 512
ROW_DMA_UNROLL = 8
GATHER_ROWS = 512
COMBINE_ROWS = 256


def _cparams(n_axes):
    return pltpu.CompilerParams(
        dimension_semantics=("arbitrary",) * n_axes,
        vmem_limit_bytes=VMEM_LIMIT_BYTES)


def _silu(x):
    return x / (1.0 + jnp.exp(-x))


def _log_sigmoid(x):
    return jnp.minimum(x, 0.0) - jnp.log1p(jnp.exp(-jnp.abs(x)))


def _rmsnorm_kernel(x_ref, g_ref, o_ref):
    x = x_ref[...]
    ms = jnp.mean(x * x, axis=-1, keepdims=True)
    o_ref[...] = (x * lax.rsqrt(ms + RMS_EPS) * g_ref[...]).astype(o_ref.dtype)


def _rmsnorm(x, g, out_dtype):
    m, d = x.shape
    tm = min(256, m)
    return pl.pallas_call(
        _rmsnorm_kernel,
        grid=(m // tm,),
        in_specs=[pl.BlockSpec((tm, d), lambda i: (i, 0)),
                  pl.BlockSpec((1, d), lambda i: (0, 0))],
        out_specs=pl.BlockSpec((tm, d), lambda i: (i, 0)),
        out_shape=jax.ShapeDtypeStruct((m, d), out_dtype),
        compiler_params=_cparams(1),
        name="rmsnorm",
    )(x, g.reshape(1, d))


def _mm_plain_kernel(x_ref, w_ref, o_ref):
    acc = jnp.dot(x_ref[...], w_ref[...].astype(BF16), preferred_element_type=F32)
    o_ref[...] = acc.astype(o_ref.dtype)


def _mm_nt_kernel(x_ref, wt_ref, o_ref):
    acc = lax.dot_general(x_ref[...], wt_ref[...].astype(BF16), (((1,), (1,)), ((), ())),
                          preferred_element_type=F32)
    o_ref[...] = acc.astype(o_ref.dtype)


def _matmul_nt(x, wt, *, n_cols, tm=None, tn=None, out_dtype=BF16, name="matmul_nt"):
    m, k = x.shape
    tm = min(tm or MM_TM, m)
    tn = min(tn or MM_TN, n_cols)
    assert m % tm == 0 and n_cols % tn == 0
    return pl.pallas_call(
        _mm_nt_kernel, grid=(m // tm, n_cols // tn),
        in_specs=[pl.BlockSpec((tm, k), lambda i, j: (i, 0)),
                  pl.BlockSpec((tn, k), lambda i, j: (j, 0))],
        out_specs=pl.BlockSpec((tm, tn), lambda i, j: (i, j)),
        out_shape=jax.ShapeDtypeStruct((m, n_cols), out_dtype),
        compiler_params=_cparams(2), name=name,
    )(x, wt)


def _mm_res_kernel(x_ref, w_ref, res_ref, o_ref):
    acc = jnp.dot(x_ref[...], w_ref[...].astype(BF16), preferred_element_type=F32)
    o_ref[...] = res_ref[...] + acc


def _mm_swiglu_kernel(x_ref, wg_ref, wu_ref, o_ref):
    x = x_ref[...]
    g = jnp.dot(x, wg_ref[...].astype(BF16), preferred_element_type=F32)
    u = jnp.dot(x, wu_ref[...].astype(BF16), preferred_element_type=F32)
    o_ref[...] = (_silu(g) * u).astype(o_ref.dtype)


def _matmul(x, w, *, n_cols, tk, k_blk=0, tm=None, tn=None,
            out_dtype=BF16, res=None, name="matmul"):
    m = x.shape[0]
    tm = min(tm or MM_TM, m)
    tn = min(tn or MM_TN, n_cols)
    assert m % tm == 0 and n_cols % tn == 0
    grid = (m // tm, n_cols // tn)
    x_spec = pl.BlockSpec((tm, tk), lambda i, j: (i, k_blk))
    w_spec = pl.BlockSpec((tk, tn), lambda i, j: (k_blk, j))
    o_spec = pl.BlockSpec((tm, tn), lambda i, j: (i, j))
    if res is None:
        kern, in_specs, args = _mm_plain_kernel, [x_spec, w_spec], (x, w)
    else:
        kern, in_specs, args = _mm_res_kernel, [x_spec, w_spec, o_spec], (x, w, res)
        out_dtype = F32
    return pl.pallas_call(
        kern, grid=grid, in_specs=in_specs, out_specs=o_spec,
        out_shape=jax.ShapeDtypeStruct((m, n_cols), out_dtype),
        compiler_params=_cparams(2), name=name,
    )(*args)


def _swiglu_in(x, w, d_ff, *, tm=None, tn=None, name="swiglu_in"):
    m, k = x.shape
    tm = min(tm or MM_TM, m)
    tn = tn or 256
    assert m % tm == 0 and d_ff % tn == 0
    nb = d_ff // tn
    return pl.pallas_call(
        _mm_swiglu_kernel,
        grid=(m // tm, nb),
        in_specs=[pl.BlockSpec((tm, k), lambda i, j: (i, 0)),
                  pl.BlockSpec((k, tn), lambda i, j: (0, j)),
                  pl.BlockSpec((k, tn), lambda i, j: (0, nb + j))],
        out_specs=pl.BlockSpec((tm, tn), lambda i, j: (i, j)),
        out_shape=jax.ShapeDtypeStruct((m, d_ff), BF16),
        compiler_params=_cparams(2), name=name,
    )(x, w, w)


def _dot_f32(a, b):
    a_hi, a_lo = _split_bf16(a, 2)
    b_hi, b_lo = _split_bf16(b, 2)
    return (jnp.dot(a_hi, b_hi, preferred_element_type=F32) + jnp.dot(a_hi, b_lo, preferred_element_type=F32)
            + jnp.dot(a_lo, b_hi, preferred_element_type=F32))


def _split_bf16(x, terms):
    out = []
    for _ in range(terms):
        t = x.astype(BF16)
        out.append(t)
        x = x - t.astype(F32)
    return out


def _dot_exact_lhs(lhs, x):
    return sum(jnp.dot(lhs, t, preferred_element_type=F32) for t in _split_bf16(x, 3))


def _gla_gate_kernel(h_ref, wa_ref, wg_ref, bg_ref, o_ref, span_ref, *, chunk, rank):
    tm, d = h_ref.shape
    w_a = jnp.concatenate([wa_ref[...].astype(BF16), jnp.zeros((LANES - rank, d), BF16)], axis=0)
    a_low = lax.dot_general(h_ref[...], w_a, (((1,), (1,)), ((), ())),
                            preferred_element_type=F32)
    xg = _dot_f32(a_low, wg_ref[...]) + bg_ref[...]
    log_alpha = _log_sigmoid(xg) * (1.0 / GLA_GATE_TAU)
    row = lax.broadcasted_iota(jnp.int32, (tm, tm), 0)
    col = lax.broadcasted_iota(jnp.int32, (tm, tm), 1)
    same_chunk = (row // chunk) == (col // chunk)
    tril = jnp.where(same_chunk & (col <= row), 1.0, 0.0).astype(BF16)
    b = _dot_exact_lhs(tril, log_alpha)
    o_ref[...] = b
    spans = [jnp.max(b[c * chunk:c * chunk + 1, :] - b[(c + 1) * chunk - 1:(c + 1) * chunk, :],
                     axis=-1, keepdims=True) for c in range(tm // chunk)]
    spans = jnp.concatenate(spans + [jnp.zeros((span_ref.shape[0] - len(spans), 1), F32)], axis=0)
    span_ref[...] = jnp.broadcast_to(spans, span_ref.shape)


def _gla_gate(h, w_in_t, n_proj, w_gate, b_gate, chunk):
    m, d = h.shape
    rank, dk = w_gate.shape
    assert n_proj % rank == 0 and rank % 16 == 0 and rank <= LANES
    tm = min(256, m)
    per_step = tm // chunk
    assert tm % chunk == 0 and per_step <= SUBLANES
    wg_pad = jnp.zeros((LANES, dk), F32).at[:rank, :].set(w_gate)
    b, spans = pl.pallas_call(
        functools.partial(_gla_gate_kernel, chunk=chunk, rank=rank),
        grid=(m // tm,),
        in_specs=[pl.BlockSpec((tm, d), lambda i: (i, 0)),
                  pl.BlockSpec((rank, d), lambda i: (n_proj // rank, 0)),
                  pl.BlockSpec((LANES, dk), lambda i: (0, 0)),
                  pl.BlockSpec((1, dk), lambda i: (0, 0))],
        out_specs=[pl.BlockSpec((tm, dk), lambda i: (i, 0)),
                   pl.BlockSpec((None, SUBLANES, LANES), lambda i: (i, 0, 0))],
        out_shape=[jax.ShapeDtypeStruct((m, dk), F32),
                   jax.ShapeDtypeStruct((m // tm, SUBLANES, LANES), F32)],
        compiler_params=_cparams(1), name="gla_gate",
    )(h, w_in_t, wg_pad, b_gate.reshape(1, dk))
    return b, spans[:, :per_step, 0].reshape(m // chunk)


def _gla_core_kernel(mild_ref, q_ref, k_ref, v_ref, r_ref, b_ref, gain_ref, o_ref, s_ref, sc_ref, *, scale, sub):
    chunk, hk = q_ref.shape
    mild_decay = mild_ref[pl.program_id(0) * pl.num_programs(2) + pl.program_id(2)] == 1

    @pl.when(pl.program_id(2) == 0)
    def _():
        s_ref[...] = jnp.zeros_like(s_ref)

    b = b_ref[...]
    q = q_ref[...].astype(F32) * scale
    k = k_ref[...].astype(F32)
    v = v_ref[...]
    state = s_ref[...]

    o = jnp.dot((q * jnp.exp(b)).astype(BF16), state.astype(BF16), preferred_element_type=F32)

    @pl.when(mild_decay)
    def _():
        b0 = b[0:1, :]
        q_t = (q * jnp.exp(b - b0)).astype(BF16)
        k_t = (k * jnp.exp(b0 - b)).astype(BF16)
        full = lax.dot_general(q_t, k_t, (((1,), (1,)), ((), ())), preferred_element_type=F32)
        t_id = lax.broadcasted_iota(jnp.int32, (chunk, chunk), 0)
        s_id = lax.broadcasted_iota(jnp.int32, (chunk, chunk), 1)
        sc_ref[...] = jnp.where(s_id <= t_id, full, 0.0)

    @pl.when(jnp.logical_not(mild_decay))
    def _():
        row_id = lax.broadcasted_iota(jnp.int32, (sub, chunk), 0)
        key_id = lax.broadcasted_iota(jnp.int32, (sub, chunk), 1)
        for i in range(chunk // sub):
            lo = i * sub
            b_i = b[lo:lo + sub, :]
            q_i = q[lo:lo + sub, :]
            diag = jnp.zeros((sub, chunk), F32)
            for j in range(sub):
                s = lo + j
                decay = jnp.exp(jnp.minimum(b_i - b[s:s + 1, :], 0.0))
                col = jnp.sum(q_i * decay * k[s:s + 1, :], axis=-1, keepdims=True)
                diag = jnp.where(key_id == s, col, diag)
            scores = jnp.where(key_id - lo <= row_id, diag, 0.0)
            if i > 0:
                b_first = b[lo:lo + 1, :]
                q_t = (q_i * jnp.exp(b_i - b_first)).astype(BF16)
                k_t = (k * jnp.exp(jnp.minimum(b_first - b, 0.0))).astype(BF16)
                below = lax.dot_general(q_t, k_t, (((1,), (1,)), ((), ())), preferred_element_type=F32)
                scores = jnp.where(key_id < lo, below, scores)
            sc_ref[lo:lo + sub, :] = scores

    o = o + jnp.dot(sc_ref[...].astype(BF16), v, preferred_element_type=F32)

    b_last = b[chunk - 1:chunk, :]
    k_state = (k * jnp.exp(b_last - b)).astype(BF16)
    update = lax.dot_general(k_state, v, (((0,), (0,)), ((), ())), preferred_element_type=F32)
    decay_rows = jnp.broadcast_to(jnp.exp(b_last), (LANES, hk))
    decay_col = jnp.transpose(decay_rows)[:, 0:1]
    s_ref[...] = state * decay_col + update

    ms = jnp.mean(o * o, axis=-1, keepdims=True)
    o = o * lax.rsqrt(ms + RMS_EPS) * gain_ref[...]
    o_ref[...] = (o * _silu(r_ref[...].astype(F32))).astype(o_ref.dtype)


def _gla_core(proj, b, chunk_decay, gain, batch, seq, heads, dk, dv):
    m = batch * seq
    hk, hv = dk // heads, dv // heads
    chunk = min(GLA_CHUNK, seq)
    nc = seq // chunk
    k_blk0 = dk // hk
    v_blk0 = (2 * dk) // hv
    r_blk0 = (2 * dk + dv) // hv
    mild = (chunk_decay < GLA_MILD_DECAY).astype(jnp.int32)

    def rows(bi, ci):
        return bi * nc + ci

    return pl.pallas_call(
        functools.partial(_gla_core_kernel, scale=float(hk) ** -0.5, sub=min(GLA_SUB, chunk)),
        grid_spec=pltpu.PrefetchScalarGridSpec(
            num_scalar_prefetch=1, grid=(batch, heads, nc),
            in_specs=[pl.BlockSpec((chunk, hk), lambda bi, h, c, mild: (rows(bi, c), h)),
                      pl.BlockSpec((chunk, hk), lambda bi, h, c, mild: (rows(bi, c), k_blk0 + h)),
                      pl.BlockSpec((chunk, hv), lambda bi, h, c, mild: (rows(bi, c), v_blk0 + h)),
                      pl.BlockSpec((chunk, hv), lambda bi, h, c, mild: (rows(bi, c), r_blk0 + h)),
                      pl.BlockSpec((chunk, hk), lambda bi, h, c, mild: (rows(bi, c), h)),
                      pl.BlockSpec((1, hv), lambda bi, h, c, mild: (0, 0))],
            out_specs=pl.BlockSpec((chunk, hv), lambda bi, h, c, mild: (rows(bi, c), h)),
            scratch_shapes=[pltpu.VMEM((hk, hv), F32), pltpu.VMEM((chunk, chunk), F32)]),
        out_shape=jax.ShapeDtypeStruct((m, dv), BF16),
        compiler_params=_cparams(3), name="gla_core",
    )(mild, proj, proj, proj, proj, b, gain.reshape(1, hv))


def _sb_kernel(q_ref, k_ref, v_ref, o_ref, acc_ref, carry_ref, hl0_ref, hl1_ref, lsp0_ref, lsp1_ref,
               sl0_ref, sl1_ref, tot0_ref, tot1_ref, *, scale, tk):
    tq, dh = q_ref.shape
    ratio = tq // tk
    assert ratio % 2 == 0
    hl_refs, lsp_refs = (hl0_ref, hl1_ref), (lsp0_ref, lsp1_ref)
    sl_refs, tot_refs = (sl0_ref, sl1_ref), (tot0_ref, tot1_ref)
    qi = pl.program_id(2)
    q = (q_ref[...].astype(F32) * (scale * LOG2_E)).astype(BF16)

    later = lax.broadcasted_iota(jnp.int32, (tk, tk), 0)
    key = lax.broadcasted_iota(jnp.int32, (tk, tk), 1)
    suffix_ones = jnp.where((later > key) | (key == tk - 1), 1.0, 0.0).astype(BF16)

    acc_ref[...] = jnp.zeros_like(acc_ref)
    carry_ref[...] = jnp.zeros_like(carry_ref)

    def first_row(t):
        return (ratio - 1 - t) * tk if t is not None and t < ratio else 0

    def strict_mask(t, r0):
        if t is None or t >= ratio:
            return None
        return (lax.broadcasted_iota(jnp.int32, (tq - r0, tk), 1)
                < lax.broadcasted_iota(jnp.int32, (tq - r0, tk), 0))

    def logit_stage(k_start, parity, t):
        r0 = first_row(t)
        strict = strict_mask(t, r0)
        k_blk = k_ref[pl.ds(k_start, tk), :]
        z = lax.dot_general(q[r0:, :], k_blk, (((1,), (1,)), ((), ())), preferred_element_type=F32)
        neg_abs = pltpu.bitcast(pltpu.bitcast(z, jnp.uint32) | jnp.uint32(0x80000000), F32)
        ls_pos = jnp.minimum(z, 0.0) - jnp.log(1.0 + jnp.exp2(neg_abs)) * LOG2_E
        log_1m = ls_pos - z
        if strict is not None:
            log_1m = jnp.where(strict, log_1m, 0.0)
        hl_refs[parity][r0:, :] = log_1m.astype(BF16)
        lsp_refs[parity][r0:, :] = ls_pos

    def suffix_stage(parity, t):
        r0 = first_row(t)
        sums = jnp.dot(hl_refs[parity][r0:, :], suffix_ones, preferred_element_type=F32)
        last_key = lax.broadcasted_iota(jnp.int32, sums.shape, 1) == tk - 1
        sl_refs[parity][r0:, :] = lsp_refs[parity][r0:, :] + jnp.where(last_key, 0.0, sums)
        tot_refs[parity][r0:, :] = jnp.broadcast_to(sums[:, tk - 1:tk], (tq - r0, LANES))

    def value_stage(k_start, parity, t):
        r0 = first_row(t)
        strict = strict_mask(t, r0)
        v_blk = v_ref[pl.ds(k_start, tk), :]
        carry = carry_ref[r0:, :]
        w = jnp.exp2(sl_refs[parity][r0:, :] + jnp.concatenate([carry] * (tk // LANES), axis=1))
        if strict is not None:
            w = jnp.where(strict, w, 0.0)
        acc_ref[r0:, :] += jnp.dot(w.astype(BF16), v_blk, preferred_element_type=F32)
        carry_ref[r0:, :] = carry + tot_refs[parity][r0:, :]

    def k_start_of(t):
        return pl.multiple_of(qi * tq + (ratio - 1 - t) * tk, tk)

    def run_step(s, parity, do_logit=True, do_suffix=True):
        prefix = isinstance(s, int)
        if do_logit:
            logit_stage(k_start_of(s + 2), parity, s + 2 if prefix else None)
        if do_suffix:
            suffix_stage(1 - parity, s + 1 if prefix else None)
        value_stage(k_start_of(s), parity, s if prefix else None)

    def prologue():
        logit_stage(k_start_of(0), 0, 0)
        logit_stage(k_start_of(1), 1, 1)
        suffix_stage(0, 0)

    @pl.when(qi == 0)
    def _():
        prologue()
        for s in range(ratio):
            run_step(s, s % 2, do_logit=s + 2 < ratio, do_suffix=s + 1 < ratio)

    @pl.when(qi > 0)
    def _():
        n_tiles = ratio * (qi + 1)
        prologue()
        for s in range(ratio):
            run_step(s, s % 2)

        def body(it, _):
            for j in range(2):
                run_step(ratio + 2 * it + j, j)
            return 0

        lax.fori_loop(0, (ratio * qi - 2) // 2, body, 0)
        run_step(n_tiles - 2, 0, do_logit=False)
        run_step(n_tiles - 1, 1, do_logit=False, do_suffix=False)

    o_ref[...] = acc_ref[...].astype(o_ref.dtype)


def _sb_core(qkv, batch, seq, heads, d_model):
    dh = d_model // heads
    tq = min(SB_TQ, seq)
    tk = min(SB_TK, tq)
    nq = seq // tq
    return pl.pallas_call(
        functools.partial(_sb_kernel, scale=float(dh) ** -0.5, tk=tk),
        grid=(batch, heads, nq),
        in_specs=[pl.BlockSpec((tq, dh), lambda b, h, i: (b * nq + i, h)),
                  pl.BlockSpec((seq, dh), lambda b, h, i: (b, heads + h)),
                  pl.BlockSpec((seq, dh), lambda b, h, i: (b, 2 * heads + h))],
        out_specs=pl.BlockSpec((tq, dh), lambda b, h, i: (b * nq + i, h)),
        out_shape=jax.ShapeDtypeStruct((batch * seq, d_model), BF16),
        scratch_shapes=[pltpu.VMEM((tq, dh), F32), pltpu.VMEM((tq, LANES), F32),
                        pltpu.VMEM((tq, tk), BF16), pltpu.VMEM((tq, tk), BF16),
                        pltpu.VMEM((tq, tk), F32), pltpu.VMEM((tq, tk), F32),
                        pltpu.VMEM((tq, tk), F32), pltpu.VMEM((tq, tk), F32),
                        pltpu.VMEM((tq, LANES), F32), pltpu.VMEM((tq, LANES), F32)],
        compiler_params=_cparams(3), name="sb_core",
    )(qkv, qkv, qkv)


def _pack_bf16_pair(hi, lo):
    hi_bits = pltpu.bitcast(hi.astype(BF16).astype(F32), jnp.uint32)
    lo_bits = pltpu.bitcast(lo.astype(BF16).astype(F32), jnp.uint32)
    return hi_bits | (lo_bits >> jnp.uint32(16))


def _unpack_bf16_pair(words):
    hi = pltpu.bitcast(words & jnp.uint32(0xFFFF0000), F32)
    lo = pltpu.bitcast(words << jnp.uint32(16), F32)
    return hi, lo


def _router_kernel(x_ref, g_ref, wr_ref, h_ref, top_ref, *, n_experts):
    x = x_ref[...]
    half = x.shape[1] // 2
    ms = jnp.mean(x * x, axis=-1, keepdims=True)
    h = x * lax.rsqrt(ms + RMS_EPS) * g_ref[...]
    h_ref[...] = _pack_bf16_pair(h[:, :half], h[:, half:])
    logits = _dot_f32(h, wr_ref[...])
    lane = lax.broadcasted_iota(jnp.int32, logits.shape, 1)
    logits = jnp.where(lane < n_experts, logits, -jnp.inf)
    v1 = jnp.max(logits, axis=-1, keepdims=True)
    i1 = jnp.min(jnp.where(logits == v1, lane, LANES), axis=-1, keepdims=True)
    rest = jnp.where(lane == i1, -jnp.inf, logits)
    v2 = jnp.max(rest, axis=-1, keepdims=True)
    i2 = jnp.min(jnp.where(rest == v2, lane, LANES), axis=-1, keepdims=True)
    e = jnp.exp(v2 - v1)
    g1 = 1.0 / (1.0 + e)
    g2 = e * g1
    out = jnp.where(lane == 0, i1.astype(F32),
                    jnp.where(lane == 1, i2.astype(F32),
                              jnp.where(lane == 2, g1, jnp.where(lane == 3, g2, 0.0))))
    top_ref[...] = out


def _router(x, g, w_router):
    m, d = x.shape
    n_experts = w_router.shape[1]
    tm = min(256, m)
    wr_pad = jnp.zeros((d, LANES), F32).at[:, :n_experts].set(w_router)
    return pl.pallas_call(
        functools.partial(_router_kernel, n_experts=n_experts),
        grid=(m // tm,),
        in_specs=[pl.BlockSpec((tm, d), lambda i: (i, 0)),
                  pl.BlockSpec((1, d), lambda i: (0, 0)),
                  pl.BlockSpec((d, LANES), lambda i: (0, 0))],
        out_specs=[pl.BlockSpec((tm, d // 2), lambda i: (i, 0)),
                   pl.BlockSpec((tm, LANES), lambda i: (i, 0))],
        out_shape=[jax.ShapeDtypeStruct((m, d // 2), jnp.uint32),
                   jax.ShapeDtypeStruct((m, LANES), F32)],
        compiler_params=_cparams(1), name="moe_router",
    )(x, g.reshape(1, d), wr_pad)


def _row_copy(src_hbm, dst_ref, sem, src_row, dst_row):
    return pltpu.make_async_copy(src_hbm.at[pl.ds(src_row, 1), :],
                                 dst_ref.at[pl.ds(dst_row, 1), :], sem)


def _gather_kernel(idx_ref, nrows_ref, src_hbm, o_ref, buf_ref, sem):
    rows = o_ref.shape[0]
    base = pl.program_id(0) * rows

    @pl.when(base < nrows_ref[0])
    def _():
        def start(r, _):
            _row_copy(src_hbm, buf_ref, sem, idx_ref[base + r], r).start()
            return 0

        def wait(r, _):
            _row_copy(src_hbm, buf_ref, sem, 0, r).wait()
            return 0

        lax.fori_loop(0, rows, start, 0, unroll=ROW_DMA_UNROLL)
        lax.fori_loop(0, rows, wait, 0, unroll=ROW_DMA_UNROLL)
        half = buf_ref.shape[1]
        hi, lo = _unpack_bf16_pair(buf_ref[...])
        o_ref[:, :half] = hi.astype(o_ref.dtype)
        o_ref[:, half:] = lo.astype(o_ref.dtype)

    @pl.when(base >= nrows_ref[0])
    def _():
        o_ref[...] = jnp.zeros_like(o_ref)


def _gather_rows(src_packed, idx, n_rows_valid):
    n = idx.shape[0]
    half = src_packed.shape[1]
    rows = min(GATHER_ROWS, n)
    assert n % rows == 0
    return pl.pallas_call(
        _gather_kernel,
        grid_spec=pltpu.PrefetchScalarGridSpec(
            num_scalar_prefetch=2, grid=(n // rows,),
            in_specs=[pl.BlockSpec(memory_space=pl.ANY)],
            out_specs=pl.BlockSpec((rows, 2 * half), lambda i, idx, nr: (i, 0)),
            scratch_shapes=[pltpu.VMEM((rows, half), jnp.uint32), pltpu.SemaphoreType.DMA(())]),
        out_shape=jax.ShapeDtypeStruct((n, 2 * half), BF16),
        compiler_params=_cparams(1), name="moe_gather",
    )(idx, n_rows_valid, src_packed)


def _weight_stream(plan_ref, copies, refill):
    j, i = pl.program_id(0), pl.program_id(1)

    @pl.when((j == 0) & (i == 0))
    def _():
        for cp in copies(plan_ref[0, 0], 0):
            cp.start()

    @pl.when(plan_ref[1, i] == 1)
    def _():
        for cp in copies(plan_ref[0, i], j):
            cp.wait()
        refill()
        j_next = j + plan_ref[3, i]

        @pl.when(j_next < pl.num_programs(0))
        def _():
            for cp in copies(plan_ref[2, i], j_next):
                cp.start()


def _moe_in_kernel(plan_ref, nvalid_ref, x_ref, w_hbm, o_ref, wf_ref, wb_ref, sem, *, d_ff, last_shift):
    j, i = pl.program_id(0), pl.program_id(1)
    tn = o_ref.shape[1]

    def copies(expert, jb):
        c = pl.multiple_of(jnp.minimum(jb * (tn // LANES), (d_ff - tn) // LANES) * LANES, LANES)
        return (pltpu.make_async_copy(w_hbm.at[expert, :, pl.ds(c, tn)], wf_ref.at[:, pl.ds(0, tn)], sem.at[0]),
                pltpu.make_async_copy(w_hbm.at[expert, :, pl.ds(pl.multiple_of(d_ff + c, LANES), tn)],
                                      wf_ref.at[:, pl.ds(tn, tn)], sem.at[1]))

    def refill():
        wb_ref[...] = wf_ref[...].astype(BF16)

    _weight_stream(plan_ref, copies, refill)

    @pl.when(i < nvalid_ref[0])
    def _():
        gu = jnp.dot(x_ref[...], wb_ref[...], preferred_element_type=F32)
        act = (_silu(gu[:, :tn]) * gu[:, tn:]).astype(o_ref.dtype)
        if last_shift == 0:
            o_ref[...] = act
        else:
            is_last = j == pl.num_programs(0) - 1

            @pl.when(is_last)
            def _():
                o_ref[:, :tn - last_shift] = act[:, last_shift:]
                o_ref[:, tn - last_shift:] = jnp.zeros((act.shape[0], last_shift), o_ref.dtype)

            @pl.when(jnp.logical_not(is_last))
            def _():
                o_ref[...] = act

    @pl.when(i >= nvalid_ref[0])
    def _():
        o_ref[...] = jnp.zeros_like(o_ref)


def _moe_in(xs, w_in, plan, n_valid, tm):
    r, d = xs.shape
    d_ff = w_in.shape[2] // 2
    tn = min(MOE_IN_TN, d_ff)
    nb = pl.cdiv(d_ff, tn)
    assert d_ff % LANES == 0 and tn % LANES == 0

    def x_map(j, i, plan, nv):
        return (jnp.minimum(i, nv[0] - 1), 0)

    return pl.pallas_call(
        functools.partial(_moe_in_kernel, d_ff=d_ff, last_shift=nb * tn - d_ff),
        grid_spec=pltpu.PrefetchScalarGridSpec(
            num_scalar_prefetch=2, grid=(nb, r // tm),
            in_specs=[pl.BlockSpec((tm, d), x_map),
                      pl.BlockSpec(memory_space=pl.ANY)],
            out_specs=pl.BlockSpec((tm, tn), lambda j, i, plan, nv: (i, j)),
            scratch_shapes=[pltpu.VMEM((d, 2 * tn), F32), pltpu.VMEM((d, 2 * tn), BF16),
                            pltpu.SemaphoreType.DMA((2,))]),
        out_shape=jax.ShapeDtypeStruct((r, nb * tn), BF16),
        compiler_params=_cparams(2), name="moe_in",
    )(plan, n_valid, xs, w_in)


def _moe_out_kernel(plan_ref, nvalid_ref, x_ref, w_hbm, o_ref, wf_ref, wb_ref, sem):
    i = pl.program_id(1)
    tn = wf_ref.shape[1]

    def copies(expert, jb):
        c = pl.multiple_of(jb * tn, LANES)
        return (pltpu.make_async_copy(w_hbm.at[expert, :, pl.ds(c, tn)], wf_ref, sem.at[0]),)

    def refill():
        wb_ref[...] = wf_ref[...].astype(BF16)

    _weight_stream(plan_ref, copies, refill)

    @pl.when(i < nvalid_ref[0])
    def _():
        y = jnp.dot(x_ref[...], wb_ref[...], preferred_element_type=F32)
        o_ref[...] = _pack_bf16_pair(y[:, :tn // 2], y[:, tn // 2:])

    @pl.when(i >= nvalid_ref[0])
    def _():
        o_ref[...] = jnp.zeros_like(o_ref)


def _moe_out(act, w_out, plan, n_valid, tm):
    r = act.shape[0]
    d_ff, d = w_out.shape[1:]
    tn = min(MOE_OUT_TN, d)
    assert d % tn == 0 and tn % (2 * LANES) == 0

    def x_map(j, i, plan, nv):
        return (jnp.minimum(i, nv[0] - 1), 0)

    return pl.pallas_call(
        _moe_out_kernel,
        grid_spec=pltpu.PrefetchScalarGridSpec(
            num_scalar_prefetch=2, grid=(d // tn, r // tm),
            in_specs=[pl.BlockSpec((tm, d_ff), x_map),
                      pl.BlockSpec(memory_space=pl.ANY)],
            out_specs=pl.BlockSpec((tm, tn // 2), lambda j, i, plan, nv: (i, j)),
            scratch_shapes=[pltpu.VMEM((d_ff, tn), F32), pltpu.VMEM((d_ff, tn), BF16),
                            pltpu.SemaphoreType.DMA((1,))]),
        out_shape=jax.ShapeDtypeStruct((r, d // 2), jnp.uint32),
        compiler_params=_cparams(2), name="moe_out",
    )(plan, n_valid, act, w_out)


def _combine_kernel(dest_ref, x_ref, y_hbm, top_ref, g_ref, o_ref, a_ref, b_ref, sem, *, group):
    rows = x_ref.shape[0]
    base = pl.program_id(0) * rows

    def start(r, _):
        _row_copy(y_hbm, a_ref, sem, dest_ref[2 * (base + r)], r).start()
        _row_copy(y_hbm, b_ref, sem, dest_ref[2 * (base + r) + 1], r).start()
        return 0

    def wait(r, _):
        _row_copy(y_hbm, a_ref, sem, 0, r).wait()
        _row_copy(y_hbm, b_ref, sem, 0, r).wait()
        return 0

    lax.fori_loop(0, rows, start, 0, unroll=ROW_DMA_UNROLL)
    lax.fori_loop(0, rows, wait, 0, unroll=ROW_DMA_UNROLL)
    top = top_ref[...]
    a_hi, a_lo = _unpack_bf16_pair(a_ref[...])
    b_hi, b_lo = _unpack_bf16_pair(b_ref[...])
    moe_hi = a_hi * top[:, TOP_K:TOP_K + 1] + b_hi * top[:, TOP_K + 1:TOP_K + 2]
    moe_lo = a_lo * top[:, TOP_K:TOP_K + 1] + b_lo * top[:, TOP_K + 1:TOP_K + 2]
    pieces = []
    for j in range(moe_hi.shape[1] // group):
        pieces += [moe_hi[:, j * group:(j + 1) * group], moe_lo[:, j * group:(j + 1) * group]]
    x = x_ref[...] + jnp.concatenate(pieces, axis=1)
    ms = jnp.mean(x * x, axis=-1, keepdims=True)
    o_ref[...] = x * lax.rsqrt(ms + RMS_EPS) * g_ref[...]


def _combine_norm(x, y, dest, top, g):
    m, d = x.shape
    rows = min(COMBINE_ROWS, m)
    return pl.pallas_call(
        functools.partial(_combine_kernel, group=min(MOE_OUT_TN, d) // 2),
        grid_spec=pltpu.PrefetchScalarGridSpec(
            num_scalar_prefetch=1, grid=(m // rows,),
            in_specs=[pl.BlockSpec((rows, d), lambda i, dest: (i, 0)),
                      pl.BlockSpec(memory_space=pl.ANY),
                      pl.BlockSpec((rows, LANES), lambda i, dest: (i, 0)),
                      pl.BlockSpec((1, d), lambda i, dest: (0, 0))],
            out_specs=pl.BlockSpec((rows, d), lambda i, dest: (i, 0)),
            scratch_shapes=[pltpu.VMEM((rows, d // 2), jnp.uint32), pltpu.VMEM((rows, d // 2), jnp.uint32),
                            pltpu.SemaphoreType.DMA(())]),
        out_shape=jax.ShapeDtypeStruct((m, d), F32),
        compiler_params=_cparams(1), name="moe_combine_norm",
    )(dest, x, y, top, g.reshape(1, d))


def _dispatch_plan(top, n_experts, tm):
    m = top.shape[0]
    n_pairs = m * TOP_K
    expert = top[:, :TOP_K].astype(jnp.int32).reshape(n_pairs)
    onehot = (expert[:, None] == jnp.arange(n_experts, dtype=jnp.int32)[None, :]).astype(jnp.int32)
    before = jnp.cumsum(onehot, axis=0) - onehot
    rank = jnp.sum(before * onehot, axis=1)
    counts = jnp.sum(onehot, axis=0)
    tiles = (counts + tm - 1) // tm
    tile_end = jnp.cumsum(tiles)
    group_start = (tile_end - tiles) * tm
    dest = group_start[expert] + rank
    n_tiles = n_pairs // tm + n_experts
    n_rows = n_tiles * tm
    src_token = jnp.zeros((n_rows,), jnp.int32).at[dest].set(jnp.arange(n_pairs, dtype=jnp.int32) // TOP_K)
    tile_id = jnp.arange(n_tiles, dtype=jnp.int32)
    tile_expert = jnp.minimum(jnp.sum((tile_end[None, :] <= tile_id[:, None]).astype(jnp.int32), axis=1),
                              n_experts - 1)
    n_valid = tile_end[-1:].astype(jnp.int32)
    prev_expert = jnp.concatenate([jnp.full((1,), -1, jnp.int32), tile_expert[:-1]])
    first = (tile_id < n_valid[0]) & (tile_expert != prev_expert)
    later_first = first[None, :] & (tile_id[None, :] > tile_id[:, None])
    next_first = jnp.min(jnp.where(later_first, tile_id[None, :], n_tiles), axis=1)
    is_last_run = next_first == n_tiles
    next_expert = jnp.where(is_last_run, tile_expert[0], tile_expert[jnp.minimum(next_first, n_tiles - 1)])
    plan = jnp.stack([tile_expert, first.astype(jnp.int32), next_expert, is_last_run.astype(jnp.int32)])
    return dest.astype(jnp.int32), src_token, plan.astype(jnp.int32), n_valid


def kernel(x, attn_norm, ffn_norm, gla_w_in, gla_w_gate, gla_b_gate, gla_onorm, gla_w_out,
           sb_w_in, sb_w_out, dense_w_in, dense_w_out, moe_router, moe_w_in, moe_w_out,
           final_norm):
    batch, seq, d = x.shape
    m = batch * seq
    x = x.reshape(m, d)

    rank, dk = gla_w_gate.shape[1:]
    hv = gla_onorm.shape[1]
    dv = GLA_HEADS * hv
    n_proj = 2 * dk + 2 * dv
    h = _rmsnorm(x, attn_norm[0], BF16)
    w_in_t = jnp.transpose(gla_w_in[0])
    proj = _matmul_nt(h, w_in_t, n_cols=n_proj, name="gla_in")
    b, chunk_decay = _gla_gate(h, w_in_t, n_proj, gla_w_gate[0], gla_b_gate[0], min(GLA_CHUNK, seq))
    o = _gla_core(proj, b, chunk_decay, gla_onorm[0], batch, seq, GLA_HEADS, dk, dv)
    x = _matmul(o, gla_w_out[0], n_cols=d, tk=dv, res=x, name="gla_out")

    d_ff = dense_w_out.shape[1]
    h = _rmsnorm(x, ffn_norm[0], BF16)
    act = _swiglu_in(h, dense_w_in[0], d_ff, name="dense_in")
    half = d_ff // 2
    x = _matmul(act, dense_w_out[0], n_cols=d, tk=half, k_blk=0, res=x, name="dense_out0")
    x = _matmul(act, dense_w_out[0], n_cols=d, tk=half, k_blk=1, res=x, name="dense_out1")

    h = _rmsnorm(x, attn_norm[1], BF16)
    qkv = _matmul(h, sb_w_in[0], n_cols=3 * d, tk=d, name="sb_in")
    o = _sb_core(qkv, batch, seq, SB_HEADS, d)
    x = _matmul(o, sb_w_out[0], n_cols=d, tk=d, res=x, name="sb_out")

    n_experts = moe_router.shape[2]
    tm = min(MOE_TM, m)
    h_packed, top = _router(x, ffn_norm[1], moe_router[0])
    dest, src_token, plan, n_valid = _dispatch_plan(top, n_experts, tm)
    xs = _gather_rows(h_packed, src_token, n_valid * tm)
    act = _moe_in(xs, moe_w_in[0], plan, n_valid, tm)
    y = _moe_out(act, moe_w_out[0], plan, n_valid, tm)
    out = _combine_norm(x, y, dest, top, final_norm)
    return out.reshape(batch, seq, d)
```

```python
import functools

import jax
import jax.numpy as jnp
from jax import lax
from jax.experimental import pallas as pl
from jax.experimental.pallas import tpu as pltpu

F32 = jnp.float32
BF16 = jnp.bfloat16

RMS_EPS = 1e-6
LOG2_E = 1.4426950408889634
GLA_HEADS = 4
GLA_GATE_TAU = 16.0
GLA_CHUNK = 256
GLA_SUB = 16
GLA_MILD_DECAY = 60.0
SB_HEADS = 32
SB_TQ = 512
SB_TK = 256
TOP_K = 2
LANES = 128
SUBLANES = 8
VMEM_LIMIT_BYTES = 56 * 1024 * 1024

MM_TM = 1024
MM_TN = 512
MOE_TM = 512
MOE_IN_TN = 512
MOE_OUT_TN = 512
ROW_DMA_UNROLL = 8
GATHER_ROWS = 512
COMBINE_ROWS = 256


def _cparams(n_axes):
    return pltpu.CompilerParams(
        dimension_semantics=("arbitrary",) * n_axes,
        vmem_limit_bytes=VMEM_LIMIT_BYTES)


def _silu(x):
    return x / (1.0 + jnp.exp(-x))


def _log_sigmoid(x):
    return jnp.minimum(x, 0.0) - jnp.log1p(jnp.exp(-jnp.abs(x)))


def _rmsnorm_kernel(x_ref, g_ref, o_ref):
    x = x_ref[...]
    ms = jnp.mean(x * x, axis=-1, keepdims=True)
    o_ref[...] = (x * lax.rsqrt(ms + RMS_EPS) * g_ref[...]).astype(o_ref.dtype)


def _rmsnorm(x, g, out_dtype):
    m, d = x.shape
    tm = min(256, m)
    return pl.pallas_call(
        _rmsnorm_kernel,
        grid=(m // tm,),
        in_specs=[pl.BlockSpec((tm, d), lambda i: (i, 0)),
                  pl.BlockSpec((1, d), lambda i: (0, 0))],
        out_specs=pl.BlockSpec((tm, d), lambda i: (i, 0)),
        out_shape=jax.ShapeDtypeStruct((m, d), out_dtype),
        compiler_params=_cparams(1),
        name="rmsnorm",
    )(x, g.reshape(1, d))


def _mm_plain_kernel(x_ref, w_ref, o_ref):
    acc = jnp.dot(x_ref[...], w_ref[...].astype(BF16), preferred_element_type=F32)
    o_ref[...] = acc.astype(o_ref.dtype)


def _mm_nt_kernel(x_ref, wt_ref, o_ref):
    acc = lax.dot_general(x_ref[...], wt_ref[...].astype(BF16), (((1,), (1,)), ((), ())),
                          preferred_element_type=F32)
    o_ref[...] = acc.astype(o_ref.dtype)


def _matmul_nt(x, wt, *, n_cols, tm=None, tn=None, out_dtype=BF16, name="matmul_nt"):
    m, k = x.shape
    tm = min(tm or MM_TM, m)
    tn = min(tn or MM_TN, n_cols)
    assert m % tm == 0 and n_cols % tn == 0
    return pl.pallas_call(
        _mm_nt_kernel, grid=(m // tm, n_cols // tn),
        in_specs=[pl.BlockSpec((tm, k), lambda i, j: (i, 0)),
                  pl.BlockSpec((tn, k), lambda i, j: (j, 0))],
        out_specs=pl.BlockSpec((tm, tn), lambda i, j: (i, j)),
        out_shape=jax.ShapeDtypeStruct((m, n_cols), out_dtype),
        compiler_params=_cparams(2), name=name,
    )(x, wt)


def _mm_res_kernel(x_ref, w_ref, res_ref, o_ref):
    acc = jnp.dot(x_ref[...], w_ref[...].astype(BF16), preferred_element_type=F32)
    o_ref[...] = res_ref[...] + acc


def _mm_swiglu_kernel(x_ref, wg_ref, wu_ref, o_ref):
    x = x_ref[...]
    g = jnp.dot(x, wg_ref[...].astype(BF16), preferred_element_type=F32)
    u = jnp.dot(x, wu_ref[...].astype(BF16), preferred_element_type=F32)
    o_ref[...] = (_silu(g) * u).astype(o_ref.dtype)


def _matmul(x, w, *, n_cols, tk, k_blk=0, tm=None, tn=None,
            out_dtype=BF16, res=None, name="matmul"):
    m = x.shape[0]
    tm = min(tm or MM_TM, m)
    tn = min(tn or MM_TN, n_cols)
    assert m % tm == 0 and n_cols % tn == 0
    grid = (m // tm, n_cols // tn)
    x_spec = pl.BlockSpec((tm, tk), lambda i, j: (i, k_blk))
    w_spec = pl.BlockSpec((tk, tn), lambda i, j: (k_blk, j))
    o_spec = pl.BlockSpec((tm, tn), lambda i, j: (i, j))
    if res is None:
        kern, in_specs, args = _mm_plain_kernel, [x_spec, w_spec], (x, w)
    else:
        kern, in_specs, args = _mm_res_kernel, [x_spec, w_spec, o_spec], (x, w, res)
        out_dtype = F32
    return pl.pallas_call(
        kern, grid=grid, in_specs=in_specs, out_specs=o_spec,
        out_shape=jax.ShapeDtypeStruct((m, n_cols), out_dtype),
        compiler_params=_cparams(2), name=name,
    )(*args)


def _swiglu_in(x, w, d_ff, *, tm=None, tn=None, name="swiglu_in"):
    m, k = x.shape
    tm = min(tm or MM_TM, m)
    tn = tn or 256
    assert m % tm == 0 and d_ff % tn == 0
    nb = d_ff // tn
    return pl.pallas_call(
        _mm_swiglu_kernel,
        grid=(m // tm, nb),
        in_specs=[pl.BlockSpec((tm, k), lambda i, j: (i, 0)),
                  pl.BlockSpec((k, tn), lambda i, j: (0, j)),
                  pl.BlockSpec((k, tn), lambda i, j: (0, nb + j))],
        out_specs=pl.BlockSpec((tm, tn), lambda i, j: (i, j)),
        out_shape=jax.ShapeDtypeStruct((m, d_ff), BF16),
        compiler_params=_cparams(2), name=name,
    )(x, w, w)


def _dot_f32(a, b):
    a_hi, a_lo = _split_bf16(a, 2)
    b_hi, b_lo = _split_bf16(b, 2)
    return (jnp.dot(a_hi, b_hi, preferred_element_type=F32) + jnp.dot(a_hi, b_lo, preferred_element_type=F32)
            + jnp.dot(a_lo, b_hi, preferred_element_type=F32))


def _split_bf16(x, terms):
    out = []
    for _ in range(terms):
        t = x.astype(BF16)
        out.append(t)
        x = x - t.astype(F32)
    return out


def _dot_exact_lhs(lhs, x):
    return sum(jnp.dot(lhs, t, preferred_element_type=F32) for t in _split_bf16(x, 3))


def _gla_gate_kernel(h_ref, wa_ref, wg_ref, bg_ref, o_ref, span_ref, *, chunk, rank):
    tm, d = h_ref.shape
    w_a = jnp.concatenate([wa_ref[...].astype(BF16), jnp.zeros((LANES - rank, d), BF16)], axis=0)
    a_low = lax.dot_general(h_ref[...], w_a, (((1,), (1,)), ((), ())),
                            preferred_element_type=F32)
    xg = _dot_f32(a_low, wg_ref[...]) + bg_ref[...]
    log_alpha = _log_sigmoid(xg) * (1.0 / GLA_GATE_TAU)
    row = lax.broadcasted_iota(jnp.int32, (tm, tm), 0)
    col = lax.broadcasted_iota(jnp.int32, (tm, tm), 1)
    same_chunk = (row // chunk) == (col // chunk)
    tril = jnp.where(same_chunk & (col <= row), 1.0, 0.0).astype(BF16)
    b = _dot_exact_lhs(tril, log_alpha)
    o_ref[...] = b
    spans = [jnp.max(b[c * chunk:c * chunk + 1, :] - b[(c + 1) * chunk - 1:(c + 1) * chunk, :],
                     axis=-1, keepdims=True) for c in range(tm // chunk)]
    spans = jnp.concatenate(spans + [jnp.zeros((span_ref.shape[0] - len(spans), 1), F32)], axis=0)
    span_ref[...] = jnp.broadcast_to(spans, span_ref.shape)


def _gla_gate(h, w_in_t, n_proj, w_gate, b_gate, chunk):
    m, d = h.shape
    rank, dk = w_gate.shape
    assert n_proj % rank == 0 and rank % 16 == 0 and rank <= LANES
    tm = min(256, m)
    per_step = tm // chunk
    assert tm % chunk == 0 and per_step <= SUBLANES
    wg_pad = jnp.zeros((LANES, dk), F32).at[:rank, :].set(w_gate)
    b, spans = pl.pallas_call(
        functools.partial(_gla_gate_kernel, chunk=chunk, rank=rank),
        grid=(m // tm,),
        in_specs=[pl.BlockSpec((tm, d), lambda i: (i, 0)),
                  pl.BlockSpec((rank, d), lambda i: (n_proj // rank, 0)),
                  pl.BlockSpec((LANES, dk), lambda i: (0, 0)),
                  pl.BlockSpec((1, dk), lambda i: (0, 0))],
        out_specs=[pl.BlockSpec((tm, dk), lambda i: (i, 0)),
                   pl.BlockSpec((None, SUBLANES, LANES), lambda i: (i, 0, 0))],
        out_shape=[jax.ShapeDtypeStruct((m, dk), F32),
                   jax.ShapeDtypeStruct((m // tm, SUBLANES, LANES), F32)],
        compiler_params=_cparams(1), name="gla_gate",
    )(h, w_in_t, wg_pad, b_gate.reshape(1, dk))
    return b, spans[:, :per_step, 0].reshape(m // chunk)


def _gla_core_kernel(mild_ref, q_ref, k_ref, v_ref, r_ref, b_ref, gain_ref, o_ref, s_ref, sc_ref, *, scale, sub):
    chunk, hk = q_ref.shape
    mild_decay = mild_ref[pl.program_id(0) * pl.num_programs(2) + pl.program_id(2)] == 1

    @pl.when(pl.program_id(2) == 0)
    def _():
        s_ref[...] = jnp.zeros_like(s_ref)

    b = b_ref[...]
    q = q_ref[...].astype(F32) * scale
    k = k_ref[...].astype(F32)
    v = v_ref[...]
    state = s_ref[...]

    o = jnp.dot((q * jnp.exp(b)).astype(BF16), state.astype(BF16), preferred_element_type=F32)

    @pl.when(mild_decay)
    def _():
        b0 = b[0:1, :]
        q_t = (q * jnp.exp(b - b0)).astype(BF16)
        k_t = (k * jnp.exp(b0 - b)).astype(BF16)
        full = lax.dot_general(q_t, k_t, (((1,), (1,)), ((), ())), preferred_element_type=F32)
        t_id = lax.broadcasted_iota(jnp.int32, (chunk, chunk), 0)
        s_id = lax.broadcasted_iota(jnp.int32, (chunk, chunk), 1)
        sc_ref[...] = jnp.where(s_id <= t_id, full, 0.0)

    @pl.when(jnp.logical_not(mild_decay))
    def _():
        row_id = lax.broadcasted_iota(jnp.int32, (sub, chunk), 0)
        key_id = lax.broadcasted_iota(jnp.int32, (sub, chunk), 1)
        for i in range(chunk // sub):
            lo = i * sub
            b_i = b[lo:lo + sub, :]
            q_i = q[lo:lo + sub, :]
            diag = jnp.zeros((sub, chunk), F32)
            for j in range(sub):
                s = lo + j
                decay = jnp.exp(jnp.minimum(b_i - b[s:s + 1, :], 0.0))
                col = jnp.sum(q_i * decay * k[s:s + 1, :], axis=-1, keepdims=True)
                diag = jnp.where(key_id == s, col, diag)
            scores = jnp.where(key_id - lo <= row_id, diag, 0.0)
            if i > 0:
                b_first = b[lo:lo + 1, :]
                q_t = (q_i * jnp.exp(b_i - b_first)).astype(BF16)
                k_t = (k * jnp.exp(jnp.minimum(b_first - b, 0.0))).astype(BF16)
                below = lax.dot_general(q_t, k_t, (((1,), (1,)), ((), ())), preferred_element_type=F32)
                scores = jnp.where(key_id < lo, below, scores)
            sc_ref[lo:lo + sub, :] = scores

    o = o + jnp.dot(sc_ref[...].astype(BF16), v, preferred_element_type=F32)

    b_last = b[chunk - 1:chunk, :]
    k_state = (k * jnp.exp(b_last - b)).astype(BF16)
    update = lax.dot_general(k_state, v, (((0,), (0,)), ((), ())), preferred_element_type=F32)
    decay_rows = jnp.broadcast_to(jnp.exp(b_last), (LANES, hk))
    decay_col = jnp.transpose(decay_rows)[:, 0:1]
    s_ref[...] = state * decay_col + update

    ms = jnp.mean(o * o, axis=-1, keepdims=True)
    o = o * lax.rsqrt(ms + RMS_EPS) * gain_ref[...]
    o_ref[...] = (o * _silu(r_ref[...].astype(F32))).astype(o_ref.dtype)


def _gla_core(proj, b, chunk_decay, gain, batch, seq, heads, dk, dv):
    m = batch * seq
    hk, hv = dk // heads, dv // heads
    chunk = min(GLA_CHUNK, seq)
    nc = seq // chunk
    k_blk0 = dk // hk
    v_blk0 = (2 * dk) // hv
    r_blk0 = (2 * dk + dv) // hv
    mild = (chunk_decay < GLA_MILD_DECAY).astype(jnp.int32)

    def rows(bi, ci):
        return bi * nc + ci

    return pl.pallas_call(
        functools.partial(_gla_core_kernel, scale=float(hk) ** -0.5, sub=min(GLA_SUB, chunk)),
        grid_spec=pltpu.PrefetchScalarGridSpec(
            num_scalar_prefetch=1, grid=(batch, heads, nc),
            in_specs=[pl.BlockSpec((chunk, hk), lambda bi, h, c, mild: (rows(bi, c), h)),
                      pl.BlockSpec((chunk, hk), lambda bi, h, c, mild: (rows(bi, c), k_blk0 + h)),
                      pl.BlockSpec((chunk, hv), lambda bi, h, c, mild: (rows(bi, c), v_blk0 + h)),
                      pl.BlockSpec((chunk, hv), lambda bi, h, c, mild: (rows(bi, c), r_blk0 + h)),
                      pl.BlockSpec((chunk, hk), lambda bi, h, c, mild: (rows(bi, c), h)),
                      pl.BlockSpec((1, hv), lambda bi, h, c, mild: (0, 0))],
            out_specs=pl.BlockSpec((chunk, hv), lambda bi, h, c, mild: (rows(bi, c), h)),
            scratch_shapes=[pltpu.VMEM((hk, hv), F32), pltpu.VMEM((chunk, chunk), F32)]),
        out_shape=jax.ShapeDtypeStruct((m, dv), BF16),
        compiler_params=_cparams(3), name="gla_core",
    )(mild, proj, proj, proj, proj, b, gain.reshape(1, hv))


def _sb_kernel(q_ref, k_ref, v_ref, o_ref, acc_ref, carry_ref, hl0_ref, hl1_ref, lsp0_ref, lsp1_ref,
               sl0_ref, sl1_ref, tot0_ref, tot1_ref, *, scale, tk):
    tq, dh = q_ref.shape
    ratio = tq // tk
    assert ratio % 2 == 0
    hl_refs, lsp_refs = (hl0_ref, hl1_ref), (lsp0_ref, lsp1_ref)
    sl_refs, tot_refs = (sl0_ref, sl1_ref), (tot0_ref, tot1_ref)
    qi = pl.program_id(2)
    q = (q_ref[...].astype(F32) * (scale * LOG2_E)).astype(BF16)

    later = lax.broadcasted_iota(jnp.int32, (tk, tk), 0)
    key = lax.broadcasted_iota(jnp.int32, (tk, tk), 1)
    suffix_ones = jnp.where((later > key) | (key == tk - 1), 1.0, 0.0).astype(BF16)

    acc_ref[...] = jnp.zeros_like(acc_ref)
    carry_ref[...] = jnp.zeros_like(carry_ref)

    def first_row(t):
        return (ratio - 1 - t) * tk if t is not None and t < ratio else 0

    def strict_mask(t, r0):
        if t is None or t >= ratio:
            return None
        return (lax.broadcasted_iota(jnp.int32, (tq - r0, tk), 1)
                < lax.broadcasted_iota(jnp.int32, (tq - r0, tk), 0))

    def logit_stage(k_start, parity, t):
        r0 = first_row(t)
        strict = strict_mask(t, r0)
        k_blk = k_ref[pl.ds(k_start, tk), :]
        z = lax.dot_general(q[r0:, :], k_blk, (((1,), (1,)), ((), ())), preferred_element_type=F32)
        neg_abs = pltpu.bitcast(pltpu.bitcast(z, jnp.uint32) | jnp.uint32(0x80000000), F32)
        ls_pos = jnp.minimum(z, 0.0) - jnp.log(1.0 + jnp.exp2(neg_abs)) * LOG2_E
        log_1m = ls_pos - z
        if strict is not None:
            log_1m = jnp.where(strict, log_1m, 0.0)
        hl_refs[parity][r0:, :] = log_1m.astype(BF16)
        lsp_refs[parity][r0:, :] = ls_pos

    def suffix_stage(parity, t):
        r0 = first_row(t)
        sums = jnp.dot(hl_refs[parity][r0:, :], suffix_ones, preferred_element_type=F32)
        last_key = lax.broadcasted_iota(jnp.int32, sums.shape, 1) == tk - 1
        sl_refs[parity][r0:, :] = lsp_refs[parity][r0:, :] + jnp.where(last_key, 0.0, sums)
        tot_refs[parity][r0:, :] = jnp.broadcast_to(sums[:, tk - 1:tk], (tq - r0, LANES))

    def value_stage(k_start, parity, t):
        r0 = first_row(t)
        strict = strict_mask(t, r0)
        v_blk = v_ref[pl.ds(k_start, tk), :]
        carry = carry_ref[r0:, :]
        w = jnp.exp2(sl_refs[parity][r0:, :] + jnp.concatenate([carry] * (tk // LANES), axis=1))
        if strict is not None:
            w = jnp.where(strict, w, 0.0)
        acc_ref[r0:, :] += jnp.dot(w.astype(BF16), v_blk, preferred_element_type=F32)
        carry_ref[r0:, :] = carry + tot_refs[parity][r0:, :]

    def k_start_of(t):
        return pl.multiple_of(qi * tq + (ratio - 1 - t) * tk, tk)

    def run_step(s, parity, do_logit=True, do_suffix=True):
        prefix = isinstance(s, int)
        if do_logit:
            logit_stage(k_start_of(s + 2), parity, s + 2 if prefix else None)
        if do_suffix:
            suffix_stage(1 - parity, s + 1 if prefix else None)
        value_stage(k_start_of(s), parity, s if prefix else None)

    def prologue():
        logit_stage(k_start_of(0), 0, 0)
        logit_stage(k_start_of(1), 1, 1)
        suffix_stage(0, 0)

    @pl.when(qi == 0)
    def _():
        prologue()
        for s in range(ratio):
            run_step(s, s % 2, do_logit=s + 2 < ratio, do_suffix=s + 1 < ratio)

    @pl.when(qi > 0)
    def _():
        n_tiles = ratio * (qi + 1)
        prologue()
        for s in range(ratio):
            run_step(s, s % 2)

        steady = ratio * qi - 2
        unroll = 4

        def body(it, _):
            for j in range(unroll):
                run_step(ratio + unroll * it + j, j % 2)
            return 0

        lax.fori_loop(0, steady // unroll, body, 0)

        @pl.when(steady % unroll == 2)
        def _():
            for j in range(2):
                run_step(ratio + unroll * (steady // unroll) + j, j)

        run_step(n_tiles - 2, 0, do_logit=False)
        run_step(n_tiles - 1, 1, do_logit=False, do_suffix=False)

    o_ref[...] = acc_ref[...].astype(o_ref.dtype)


def _sb_core(qkv, batch, seq, heads, d_model):
    dh = d_model // heads
    tq = min(SB_TQ, seq)
    tk = min(SB_TK, tq)
    nq = seq // tq
    return pl.pallas_call(
        functools.partial(_sb_kernel, scale=float(dh) ** -0.5, tk=tk),
        grid=(batch, heads, nq),
        in_specs=[pl.BlockSpec((tq, dh), lambda b, h, i: (b * nq + i, h)),
                  pl.BlockSpec((seq, dh), lambda b, h, i: (b, heads + h)),
                  pl.BlockSpec((seq, dh), lambda b, h, i: (b, 2 * heads + h))],
        out_specs=pl.BlockSpec((tq, dh), lambda b, h, i: (b * nq + i, h)),
        out_shape=jax.ShapeDtypeStruct((batch * seq, d_model), BF16),
        scratch_shapes=[pltpu.VMEM((tq, dh), F32), pltpu.VMEM((tq, LANES), F32),
                        pltpu.VMEM((tq, tk), BF16), pltpu.VMEM((tq, tk), BF16),
                        pltpu.VMEM((tq, tk), F32), pltpu.VMEM((tq, tk), F32),
                        pltpu.VMEM((tq, tk), F32), pltpu.VMEM((tq, tk), F32),
                        pltpu.VMEM((tq, LANES), F32), pltpu.VMEM((tq, LANES), F32)],
        compiler_params=_cparams(3), name="sb_core",
    )(qkv, qkv, qkv)


def _pack_bf16_pair(hi, lo):
    hi_bits = pltpu.bitcast(hi.astype(BF16).astype(F32), jnp.uint32)
    lo_bits = pltpu.bitcast(lo.astype(BF16).astype(F32), jnp.uint32)
    return hi_bits | (lo_bits >> jnp.uint32(16))


def _unpack_bf16_pair(words):
    hi = pltpu.bitcast(words & jnp.uint32(0xFFFF0000), F32)
    lo = pltpu.bitcast(words << jnp.uint32(16), F32)
    return hi, lo


def _router_kernel(x_ref, g_ref, wr_ref, h_ref, top_ref, *, n_experts):
    x = x_ref[...]
    half = x.shape[1] // 2
    ms = jnp.mean(x * x, axis=-1, keepdims=True)
    h = x * lax.rsqrt(ms + RMS_EPS) * g_ref[...]
    h_ref[...] = _pack_bf16_pair(h[:, :half], h[:, half:])
    logits = _dot_f32(h, wr_ref[...])
    lane = lax.broadcasted_iota(jnp.int32, logits.shape, 1)
    logits = jnp.where(lane < n_experts, logits, -jnp.inf)
    v1 = jnp.max(logits, axis=-1, keepdims=True)
    i1 = jnp.min(jnp.where(logits == v1, lane, LANES), axis=-1, keepdims=True)
    rest = jnp.where(lane == i1, -jnp.inf, logits)
    v2 = jnp.max(rest, axis=-1, keepdims=True)
    i2 = jnp.min(jnp.where(rest == v2, lane, LANES), axis=-1, keepdims=True)
    e = jnp.exp(v2 - v1)
    g1 = 1.0 / (1.0 + e)
    g2 = e * g1
    out = jnp.where(lane == 0, i1.astype(F32),
                    jnp.where(lane == 1, i2.astype(F32),
                              jnp.where(lane == 2, g1, jnp.where(lane == 3, g2, 0.0))))
    top_ref[...] = out


def _router(x, g, w_router):
    m, d = x.shape
    n_experts = w_router.shape[1]
    tm = min(256, m)
    wr_pad = jnp.zeros((d, LANES), F32).at[:, :n_experts].set(w_router)
    return pl.pallas_call(
        functools.partial(_router_kernel, n_experts=n_experts),
        grid=(m // tm,),
        in_specs=[pl.BlockSpec((tm, d), lambda i: (i, 0)),
                  pl.BlockSpec((1, d), lambda i: (0, 0)),
                  pl.BlockSpec((d, LANES), lambda i: (0, 0))],
        out_specs=[pl.BlockSpec((tm, d // 2), lambda i: (i, 0)),
                   pl.BlockSpec((tm, LANES), lambda i: (i, 0))],
        out_shape=[jax.ShapeDtypeStruct((m, d // 2), jnp.uint32),
                   jax.ShapeDtypeStruct((m, LANES), F32)],
        compiler_params=_cparams(1), name="moe_router",
    )(x, g.reshape(1, d), wr_pad)


def _row_copy(src_hbm, dst_ref, sem, src_row, dst_row):
    return pltpu.make_async_copy(src_hbm.at[pl.ds(src_row, 1), :],
                                 dst_ref.at[pl.ds(dst_row, 1), :], sem)


def _gather_kernel(idx_ref, nrows_ref, src_hbm, o_ref, buf_ref, sem):
    rows = o_ref.shape[0]
    base = pl.program_id(0) * rows

    @pl.when(base < nrows_ref[0])
    def _():
        def start(r, _):
            _row_copy(src_hbm, buf_ref, sem, idx_ref[base + r], r).start()
            return 0

        def wait(r, _):
            _row_copy(src_hbm, buf_ref, sem, 0, r).wait()
            return 0

        lax.fori_loop(0, rows, start, 0, unroll=ROW_DMA_UNROLL)
        lax.fori_loop(0, rows, wait, 0, unroll=ROW_DMA_UNROLL)
        half = buf_ref.shape[1]
        hi, lo = _unpack_bf16_pair(buf_ref[...])
        o_ref[:, :half] = hi.astype(o_ref.dtype)
        o_ref[:, half:] = lo.astype(o_ref.dtype)

    @pl.when(base >= nrows_ref[0])
    def _():
        o_ref[...] = jnp.zeros_like(o_ref)


def _gather_rows(src_packed, idx, n_rows_valid):
    n = idx.shape[0]
    half = src_packed.shape[1]
    rows = min(GATHER_ROWS, n)
    assert n % rows == 0
    return pl.pallas_call(
        _gather_kernel,
        grid_spec=pltpu.PrefetchScalarGridSpec(
            num_scalar_prefetch=2, grid=(n // rows,),
            in_specs=[pl.BlockSpec(memory_space=pl.ANY)],
            out_specs=pl.BlockSpec((rows, 2 * half), lambda i, idx, nr: (i, 0)),
            scratch_shapes=[pltpu.VMEM((rows, half), jnp.uint32), pltpu.SemaphoreType.DMA(())]),
        out_shape=jax.ShapeDtypeStruct((n, 2 * half), BF16),
        compiler_params=_cparams(1), name="moe_gather",
    )(idx, n_rows_valid, src_packed)


def _weight_stream(plan_ref, copies, refill):
    j, i = pl.program_id(0), pl.program_id(1)

    @pl.when((j == 0) & (i == 0))
    def _():
        for cp in copies(plan_ref[0, 0], 0):
            cp.start()

    @pl.when(plan_ref[1, i] == 1)
    def _():
        for cp in copies(plan_ref[0, i], j):
            cp.wait()
        refill()
        j_next = j + plan_ref[3, i]

        @pl.when(j_next < pl.num_programs(0))
        def _():
            for cp in copies(plan_ref[2, i], j_next):
                cp.start()


def _moe_in_kernel(plan_ref, nvalid_ref, x_ref, w_hbm, o_ref, wf_ref, wb_ref, sem, *, d_ff, last_shift):
    j, i = pl.program_id(0), pl.program_id(1)
    tn = o_ref.shape[1]

    def copies(expert, jb):
        c = pl.multiple_of(jnp.minimum(jb * (tn // LANES), (d_ff - tn) // LANES) * LANES, LANES)
        return (pltpu.make_async_copy(w_hbm.at[expert, :, pl.ds(c, tn)], wf_ref.at[:, pl.ds(0, tn)], sem.at[0]),
                pltpu.make_async_copy(w_hbm.at[expert, :, pl.ds(pl.multiple_of(d_ff + c, LANES), tn)],
                                      wf_ref.at[:, pl.ds(tn, tn)], sem.at[1]))

    def refill():
        wb_ref[...] = wf_ref[...].astype(BF16)

    _weight_stream(plan_ref, copies, refill)

    @pl.when(i < nvalid_ref[0])
    def _():
        gu = jnp.dot(x_ref[...], wb_ref[...], preferred_element_type=F32)
        act = (_silu(gu[:, :tn]) * gu[:, tn:]).astype(o_ref.dtype)
        if last_shift == 0:
            o_ref[...] = act
        else:
            is_last = j == pl.num_programs(0) - 1

            @pl.when(is_last)
            def _():
                o_ref[:, :tn - last_shift] = act[:, last_shift:]
                o_ref[:, tn - last_shift:] = jnp.zeros((act.shape[0], last_shift), o_ref.dtype)

            @pl.when(jnp.logical_not(is_last))
            def _():
                o_ref[...] = act

    @pl.when(i >= nvalid_ref[0])
    def _():
        o_ref[...] = jnp.zeros_like(o_ref)


def _moe_in(xs, w_in, plan, n_valid, tm):
    r, d = xs.shape
    d_ff = w_in.shape[2] // 2
    tn = min(MOE_IN_TN, d_ff)
    nb = pl.cdiv(d_ff, tn)
    assert d_ff % LANES == 0 and tn % LANES == 0

    def x_map(j, i, plan, nv):
        return (jnp.minimum(i, nv[0] - 1), 0)

    return pl.pallas_call(
        functools.partial(_moe_in_kernel, d_ff=d_ff, last_shift=nb * tn - d_ff),
        grid_spec=pltpu.PrefetchScalarGridSpec(
            num_scalar_prefetch=2, grid=(nb, r // tm),
            in_specs=[pl.BlockSpec((tm, d), x_map),
                      pl.BlockSpec(memory_space=pl.ANY)],
            out_specs=pl.BlockSpec((tm, tn), lambda j, i, plan, nv: (i, j)),
            scratch_shapes=[pltpu.VMEM((d, 2 * tn), F32), pltpu.VMEM((d, 2 * tn), BF16),
                            pltpu.SemaphoreType.DMA((2,))]),
        out_shape=jax.ShapeDtypeStruct((r, nb * tn), BF16),
        compiler_params=_cparams(2), name="moe_in",
    )(plan, n_valid, xs, w_in)


def _moe_out_kernel(plan_ref, nvalid_ref, x_ref, w_hbm, o_ref, wf_ref, wb_ref, sem):
    i = pl.program_id(1)
    tn = wf_ref.shape[1]

    def copies(expert, jb):
        c = pl.multiple_of(jb * tn, LANES)
        return (pltpu.make_async_copy(w_hbm.at[expert, :, pl.ds(c, tn)], wf_ref, sem.at[0]),)

    def refill():
        wb_ref[...] = wf_ref[...].astype(BF16)

    _weight_stream(plan_ref, copies, refill)

    @pl.when(i < nvalid_ref[0])
    def _():
        y = jnp.dot(x_ref[...], wb_ref[...], preferred_element_type=F32)
        o_ref[...] = _pack_bf16_pair(y[:, :tn // 2], y[:, tn // 2:])

    @pl.when(i >= nvalid_ref[0])
    def _():
        o_ref[...] = jnp.zeros_like(o_ref)


def _moe_out(act, w_out, plan, n_valid, tm):
    r = act.shape[0]
    d_ff, d = w_out.shape[1:]
    tn = min(MOE_OUT_TN, d)
    assert d % tn == 0 and tn % (2 * LANES) == 0

    def x_map(j, i, plan, nv):
        return (jnp.minimum(i, nv[0] - 1), 0)

    return pl.pallas_call(
        _moe_out_kernel,
        grid_spec=pltpu.PrefetchScalarGridSpec(
            num_scalar_prefetch=2, grid=(d // tn, r // tm),
            in_specs=[pl.BlockSpec((tm, d_ff), x_map),
                      pl.BlockSpec(memory_space=pl.ANY)],
            out_specs=pl.BlockSpec((tm, tn // 2), lambda j, i, plan, nv: (i, j)),
            scratch_shapes=[pltpu.VMEM((d_ff, tn), F32), pltpu.VMEM((d_ff, tn), BF16),
                            pltpu.SemaphoreType.DMA((1,))]),
        out_shape=jax.ShapeDtypeStruct((r, d // 2), jnp.uint32),
        compiler_params=_cparams(2), name="moe_out",
    )(plan, n_valid, act, w_out)


def _combine_kernel(dest_ref, x_ref, y_hbm, top_ref, g_ref, o_ref, a_ref, b_ref, sem, *, group):
    rows = x_ref.shape[0]
    base = pl.program_id(0) * rows

    def start(r, _):
        _row_copy(y_hbm, a_ref, sem, dest_ref[2 * (base + r)], r).start()
        _row_copy(y_hbm, b_ref, sem, dest_ref[2 * (base + r) + 1], r).start()
        return 0

    def wait(r, _):
        _row_copy(y_hbm, a_ref, sem, 0, r).wait()
        _row_copy(y_hbm, b_ref, sem, 0, r).wait()
        return 0

    lax.fori_loop(0, rows, start, 0, unroll=ROW_DMA_UNROLL)
    lax.fori_loop(0, rows, wait, 0, unroll=ROW_DMA_UNROLL)
    top = top_ref[...]
    a_hi, a_lo = _unpack_bf16_pair(a_ref[...])
    b_hi, b_lo = _unpack_bf16_pair(b_ref[...])
    moe_hi = a_hi * top[:, TOP_K:TOP_K + 1] + b_hi * top[:, TOP_K + 1:TOP_K + 2]
    moe_lo = a_lo * top[:, TOP_K:TOP_K + 1] + b_lo * top[:, TOP_K + 1:TOP_K + 2]
    pieces = []
    for j in range(moe_hi.shape[1] // group):
        pieces += [moe_hi[:, j * group:(j + 1) * group], moe_lo[:, j * group:(j + 1) * group]]
    x = x_ref[...] + jnp.concatenate(pieces, axis=1)
    ms = jnp.mean(x * x, axis=-1, keepdims=True)
    o_ref[...] = x * lax.rsqrt(ms + RMS_EPS) * g_ref[...]


def _combine_norm(x, y, dest, top, g):
    m, d = x.shape
    rows = min(COMBINE_ROWS, m)
    return pl.pallas_call(
        functools.partial(_combine_kernel, group=min(MOE_OUT_TN, d) // 2),
        grid_spec=pltpu.PrefetchScalarGridSpec(
            num_scalar_prefetch=1, grid=(m // rows,),
            in_specs=[pl.BlockSpec((rows, d), lambda i, dest: (i, 0)),
                      pl.BlockSpec(memory_space=pl.ANY),
                      pl.BlockSpec((rows, LANES), lambda i, dest: (i, 0)),
                      pl.BlockSpec((1, d), lambda i, dest: (0, 0))],
            out_specs=pl.BlockSpec((rows, d), lambda i, dest: (i, 0)),
            scratch_shapes=[pltpu.VMEM((rows, d // 2), jnp.uint32), pltpu.VMEM((rows, d // 2), jnp.uint32),
                            pltpu.SemaphoreType.DMA(())]),
        out_shape=jax.ShapeDtypeStruct((m, d), F32),
        compiler_params=_cparams(1), name="moe_combine_norm",
    )(dest, x, y, top, g.reshape(1, d))


def _dispatch_plan(top, n_experts, tm):
    m = top.shape[0]
    n_pairs = m * TOP_K
    expert = top[:, :TOP_K].astype(jnp.int32).reshape(n_pairs)
    onehot = (expert[:, None] == jnp.arange(n_experts, dtype=jnp.int32)[None, :]).astype(jnp.int32)
    before = jnp.cumsum(onehot, axis=0) - onehot
    rank = jnp.sum(before * onehot, axis=1)
    counts = jnp.sum(onehot, axis=0)
    tiles = (counts + tm - 1) // tm
    tile_end = jnp.cumsum(tiles)
    group_start = (tile_end - tiles) * tm
    dest = group_start[expert] + rank
    n_tiles = n_pairs // tm + n_experts
    n_rows = n_tiles * tm
    src_token = jnp.zeros((n_rows,), jnp.int32).at[dest].set(jnp.arange(n_pairs, dtype=jnp.int32) // TOP_K)
    tile_id = jnp.arange(n_tiles, dtype=jnp.int32)
    tile_expert = jnp.minimum(jnp.sum((tile_end[None, :] <= tile_id[:, None]).astype(jnp.int32), axis=1),
                              n_experts - 1)
    n_valid = tile_end[-1:].astype(jnp.int32)
    prev_expert = jnp.concatenate([jnp.full((1,), -1, jnp.int32), tile_expert[:-1]])
    first = (tile_id < n_valid[0]) & (tile_expert != prev_expert)
    later_first = first[None, :] & (tile_id[None, :] > tile_id[:, None])
    next_first = jnp.min(jnp.where(later_first, tile_id[None, :], n_tiles), axis=1)
    is_last_run = next_first == n_tiles
    next_expert = jnp.where(is_last_run, tile_expert[0], tile_expert[jnp.minimum(next_first, n_tiles - 1)])
    plan = jnp.stack([tile_expert, first.astype(jnp.int32), next_expert, is_last_run.astype(jnp.int32)])
    return dest.astype(jnp.int32), src_token, plan.astype(jnp.int32), n_valid


def kernel(x, attn_norm, ffn_norm, gla_w_in, gla_w_gate, gla_b_gate, gla_onorm, gla_w_out,
           sb_w_in, sb_w_out, dense_w_in, dense_w_out, moe_router, moe_w_in, moe_w_out,
           final_norm):
    batch, seq, d = x.shape
    m = batch * seq
    x = x.reshape(m, d)

    rank, dk = gla_w_gate.shape[1:]
    hv = gla_onorm.shape[1]
    dv = GLA_HEADS * hv
    n_proj = 2 * dk + 2 * dv
    h = _rmsnorm(x, attn_norm[0], BF16)
    w_in_t = jnp.transpose(gla_w_in[0])
    proj = _matmul_nt(h, w_in_t, n_cols=n_proj, name="gla_in")
    b, chunk_decay = _gla_gate(h, w_in_t, n_proj, gla_w_gate[0], gla_b_gate[0], min(GLA_CHUNK, seq))
    o = _gla_core(proj, b, chunk_decay, gla_onorm[0], batch, seq, GLA_HEADS, dk, dv)
    x = _matmul(o, gla_w_out[0], n_cols=d, tk=dv, res=x, name="gla_out")

    d_ff = dense_w_out.shape[1]
    h = _rmsnorm(x, ffn_norm[0], BF16)
    act = _swiglu_in(h, dense_w_in[0], d_ff, name="dense_in")
    half = d_ff // 2
    x = _matmul(act, dense_w_out[0], n_cols=d, tk=half, k_blk=0, res=x, name="dense_out0")
    x = _matmul(act, dense_w_out[0], n_cols=d, tk=half, k_blk=1, res=x, name="dense_out1")

    h = _rmsnorm(x, attn_norm[1], BF16)
    qkv = _matmul(h, sb_w_in[0], n_cols=3 * d, tk=d, name="sb_in")
    o = _sb_core(qkv, batch, seq, SB_HEADS, d)
    x = _matmul(o, sb_w_out[0], n_cols=d, tk=d, res=x, name="sb_out")

    n_experts = moe_router.shape[2]
    tm = min(MOE_TM, m)
    h_packed, top = _router(x, ffn_norm[1], moe_router[0])
    dest, src_token, plan, n_valid = _dispatch_plan(top, n_experts, tm)
    xs = _gather_rows(h_packed, src_token, n_valid * tm)
    act = _moe_in(xs, moe_w_in[0], plan, n_valid, tm)
    y = _moe_out(act, moe_w_out[0], plan, n_valid, tm)
    out = _combine_norm(x, y, dest, top, final_norm)
    return out.reshape(batch, seq, d)
```
